```python
import math
import jax, jax.numpy as jnp
from jax import lax
import numpy as np

D_MODEL = 1024
BATCH = 8
SEQ = 4096
DEPTH = 2

N_MEM = 256
GM_GROUPS = 4
GM_DIM = 128
GM_CHUNK = 128
GM_WIDTH = GM_GROUPS * GM_DIM
DN_HEADS = 4
DN_DK = 128
DN_DV = 128
DN_CHUNK = 64
CONV_W = 4
DN_QK = DN_HEADS * DN_DK
DN_V = DN_HEADS * DN_DV
MIX_WIDTH = GM_WIDTH + DN_V
D_FF = 4 * D_MODEL
XA_HEADS = 4
XA_DIM = D_MODEL // XA_HEADS
EPS = 1e-6
IN_SIZES = (GM_WIDTH, GM_WIDTH, DN_QK, DN_QK, DN_V, DN_V, DN_HEADS, DN_HEADS)
D_IN = 2 * GM_WIDTH + 2 * DN_QK + 2 * DN_V + 2 * DN_HEADS

kernel_name = "hybrid_sgu_gdn_memxattn_block"


def rmsnorm(x, g):
    xf = x.astype(jnp.float32)
    y = xf * lax.rsqrt(jnp.mean(xf * xf, axis=-1, keepdims=True) + EPS)
    return (y * g.astype(jnp.float32)).astype(x.dtype)


def layernorm(x, g, b):
    xf = x.astype(jnp.float32)
    mu = jnp.mean(xf, axis=-1, keepdims=True)
    var = jnp.mean(jnp.square(xf - mu), axis=-1, keepdims=True)
    y = (xf - mu) * lax.rsqrt(var + EPS)
    return (y * g.astype(jnp.float32) + b.astype(jnp.float32)).astype(x.dtype)


def l2norm(x):
    return x * lax.rsqrt(jnp.sum(x * x, axis=-1, keepdims=True) + EPS)


def split_cols(t, sizes):
    idx = [int(i) for i in np.cumsum(np.array(sizes))[:-1]]
    return jnp.split(t, idx, axis=-1)


def chunk_spatial_gating(u, v, ln_g, ln_b, w_s, b_s):
    bsz, s, _ = u.shape
    nc = s // GM_CHUNK
    u = jax.nn.gelu(u).reshape(bsz, nc, GM_CHUNK, GM_GROUPS, GM_DIM)
    v = jax.nn.gelu(v).reshape(bsz, nc, GM_CHUNK, GM_GROUPS, GM_DIM)
    v = layernorm(v, ln_g, ln_b)
    causal = jnp.tril(jnp.ones((GM_CHUNK, GM_CHUNK), dtype=bool))
    w = jnp.where(causal, w_s, jnp.zeros_like(w_s))
    mixed = jnp.einsum('gts,bcsgd->bctgd', w, v) + jnp.transpose(b_s)[None, None, :, :, None]
    return (u * mixed).reshape(bsz, s, GM_WIDTH)


def causal_dwconv(x, w):
    c = x.shape[-1]
    return lax.conv_general_dilated(
        x, w[:, None, :].astype(x.dtype), window_strides=(1,), padding=[(CONV_W - 1, 0)],
        dimension_numbers=('NWC', 'WIO', 'NWC'), feature_group_count=c)


def gated_delta_rule(q, k, v, g, beta):
    f32 = jnp.float32
    bsz, s, h, dk = q.shape
    dv = v.shape[-1]
    c = DN_CHUNK
    n = s // c

    def chunks(t):
        return t.astype(f32).reshape(bsz, n, c, h, -1).transpose(0, 3, 1, 2, 4)

    q = chunks(q) * (dk ** -0.5)
    k = chunks(k)
    v = chunks(v)
    beta = beta.astype(f32).reshape(bsz, n, c, h).transpose(0, 3, 1, 2)[..., None]
    g = g.astype(f32).reshape(bsz, n, c, h).transpose(0, 3, 1, 2)
    decay = jnp.cumsum(g, axis=-1)
    tri = jnp.tril(jnp.ones((c, c), dtype=bool))
    strict = jnp.tril(jnp.ones((c, c), dtype=bool), -1)
    lmask = jnp.exp(jnp.where(tri, decay[..., :, None] - decay[..., None, :], -jnp.inf))

    k_beta = k * beta
    v_beta = v * beta
    m = jnp.where(strict, jnp.einsum('bhnid,bhnjd->bhnij', k_beta, k) * lmask, 0.0)
    a = m + jnp.eye(c, dtype=f32)
    rhs = jnp.concatenate([v_beta, k_beta * jnp.exp(decay)[..., None]], axis=-1)
    sol = lax.linalg.triangular_solve(a, rhs, left_side=True, lower=True, unit_diagonal=True)
    u_c = sol[..., :dv]
    w_c = sol[..., dv:]

    qk = jnp.where(tri, jnp.einsum('bhnid,bhnjd->bhnij', q, k) * lmask, 0.0)
    q_dec = q * jnp.exp(decay)[..., None]
    k_dec = k * jnp.exp(decay[..., -1:] - decay)[..., None]
    chunk_decay = jnp.exp(decay[..., -1])

    xs = tuple(jnp.moveaxis(t, 2, 0) for t in (qk, q_dec, k_dec, u_c, w_c, chunk_decay))

    def step(state, inp):
        qk_i, q_dec_i, k_dec_i, u_i, w_i, cd_i = inp
        v_new = u_i - jnp.einsum('bhck,bhkv->bhcv', w_i, state)
        o = jnp.einsum('bhck,bhkv->bhcv', q_dec_i, state) + jnp.einsum('bhij,bhjv->bhiv', qk_i, v_new)
        state = state * cd_i[..., None, None] + jnp.einsum('bhck,bhcv->bhkv', k_dec_i, v_new)
        return state, o

    s0 = jnp.zeros((bsz, h, dk, dv), f32)
    _, o = lax.scan(step, s0, xs)
    return o.transpose(1, 0, 3, 2, 4).reshape(bsz, s, h, dv)


def hybrid_mixer(h, w_in, gm_ln_g, gm_ln_b, gm_ws, gm_bs, dn_conv, dn_a_log, dn_dt_bias, dn_onorm, w_out):
    bsz, s, _ = h.shape
    proj = h @ w_in
    gu, gv, q, k, v, gate, b_logit, a_logit = split_cols(proj, IN_SIZES)

    y_a = chunk_spatial_gating(gu, gv, gm_ln_g, gm_ln_b, gm_ws, gm_bs)

    qkv = jax.nn.silu(causal_dwconv(jnp.concatenate([q, k, v], axis=-1), dn_conv))
    q, k, v = split_cols(qkv, (DN_QK, DN_QK, DN_V))
    q = l2norm(q.astype(jnp.float32).reshape(bsz, s, DN_HEADS, DN_DK))
    k = l2norm(k.astype(jnp.float32).reshape(bsz, s, DN_HEADS, DN_DK))
    v = v.reshape(bsz, s, DN_HEADS, DN_DV)
    beta = jax.nn.sigmoid(b_logit.astype(jnp.float32))
    g = -jnp.exp(dn_a_log.astype(jnp.float32)) * jax.nn.softplus(
        a_logit.astype(jnp.float32) + dn_dt_bias.astype(jnp.float32))
    o = gated_delta_rule(q, k, v, g, beta)
    o = rmsnorm(o, dn_onorm) * jax.nn.silu(gate.astype(jnp.float32).reshape(bsz, s, DN_HEADS, DN_DV))
    y_b = o.reshape(bsz, s, DN_V).astype(h.dtype)

    return jnp.concatenate([y_a, y_b], axis=-1) @ w_out


def memory_cross_attention(h, m, w_q, w_k, w_v, w_o):
    bsz, s, _ = h.shape
    nm = m.shape[1]
    q = (h @ w_q).reshape(bsz, s, XA_HEADS, XA_DIM)
    k = (m @ w_k).reshape(bsz, nm, XA_HEADS, XA_DIM)
    v = (m @ w_v).reshape(bsz, nm, XA_HEADS, XA_DIM)
    scores = jnp.einsum('bshd,bmhd->bhsm', q, k).astype(jnp.float32) * (XA_DIM ** -0.5)
    p = jax.nn.softmax(scores, axis=-1).astype(v.dtype)
    o = jnp.einsum('bhsm,bmhd->bshd', p, v).reshape(bsz, s, D_MODEL)
    return o @ w_o


def squared_relu_mlp(h, w1, w2):
    return jnp.square(jax.nn.relu(h @ w1)) @ w2


def setup_inputs(seed: int = 0) -> dict:
    key = jax.random.key(seed)
    ks = jax.random.split(key, 26)
    f32 = jnp.float32
    L = DEPTH

    def nrm(k, shape, scale):
        return jax.random.normal(k, shape, f32) * scale

    def gain(k, shape):
        return 1.0 + 0.02 * jax.random.normal(k, shape, f32)

    dt = jnp.exp(jax.random.uniform(ks[11], (L, DN_HEADS), f32, math.log(1e-3), math.log(1e-1)))
    return {
        "x": nrm(ks[0], (BATCH, SEQ, D_MODEL), 1.0),
        "mem": nrm(ks[1], (BATCH, N_MEM, D_MODEL), 1.0),
        "norm_mix": gain(ks[2], (L, D_MODEL)),
        "w_in": nrm(ks[3], (L, D_MODEL, D_IN), D_MODEL ** -0.5),
        "gm_ln_g": gain(ks[4], (L, GM_GROUPS, GM_DIM)),
        "gm_ln_b": nrm(ks[5], (L, GM_GROUPS, GM_DIM), 0.02),
        "gm_ws": nrm(ks[6], (L, GM_GROUPS, GM_CHUNK, GM_CHUNK), GM_CHUNK ** -0.5),
        "gm_bs": gain(ks[7], (L, GM_GROUPS, GM_CHUNK)),
        "dn_conv": nrm(ks[8], (L, CONV_W, 2 * DN_QK + DN_V), CONV_W ** -0.5),
        "dn_a_log": jnp.log(jax.random.uniform(ks[9], (L, DN_HEADS), f32, 1.0, 16.0)),
        "dn_dt_bias": dt + jnp.log(-jnp.expm1(-dt)),
        "dn_onorm": gain(ks[10], (L, DN_DV)),
        "w_out": nrm(ks[12], (L, MIX_WIDTH, D_MODEL), MIX_WIDTH ** -0.5),
        "norm_xa": gain(ks[13], (L, D_MODEL)),
        "norm_mem": gain(ks[14], (L, D_MODEL)),
        "xa_wq": nrm(ks[15], (L, D_MODEL, D_MODEL), D_MODEL ** -0.5),
        "xa_wk": nrm(ks[16], (L, D_MODEL, D_MODEL), D_MODEL ** -0.5),
        "xa_wv": nrm(ks[17], (L, D_MODEL, D_MODEL), D_MODEL ** -0.5),
        "xa_wo": nrm(ks[18], (L, D_MODEL, D_MODEL), D_MODEL ** -0.5),
        "norm_ffn": gain(ks[19], (L, D_MODEL)),
        "ffn_w1": nrm(ks[20], (L, D_MODEL, D_FF), D_MODEL ** -0.5),
        "ffn_w2": nrm(ks[21], (L, D_FF, D_MODEL), D_FF ** -0.5),
        "norm_final": gain(ks[22], (D_MODEL,)),
    }


def reference(x, mem, norm_mix, w_in, gm_ln_g, gm_ln_b, gm_ws, gm_bs, dn_conv, dn_a_log, dn_dt_bias,
              dn_onorm, w_out, norm_xa, norm_mem, xa_wq, xa_wk, xa_wv, xa_wo, norm_ffn, ffn_w1, ffn_w2,
              norm_final):
    for l in range(DEPTH):
        h = rmsnorm(x, norm_mix[l])
        x = x + hybrid_mixer(h, w_in[l], gm_ln_g[l], gm_ln_b[l], gm_ws[l], gm_bs[l], dn_conv[l],
                             dn_a_log[l], dn_dt_bias[l], dn_onorm[l], w_out[l])
        m = rmsnorm(mem, norm_mem[l])
        x = x + memory_cross_attention(rmsnorm(x, norm_xa[l]), m, xa_wq[l], xa_wk[l], xa_wv[l], xa_wo[l])
        x = x + squared_relu_mlp(rmsnorm(x, norm_ffn[l]), ffn_w1[l], ffn_w2[l])
    return rmsnorm(x, norm_final)
```

```python
import functools
import math

import jax
import jax.numpy as jnp
from jax import lax
from jax.experimental import pallas as pl
from jax.experimental.pallas import tpu as pltpu

F32 = jnp.float32
BF16 = jnp.bfloat16

EPS = 1e-6
LANES = 128
CHUNK = 128
INV_BASE = 8
N_GROUPS = 4
N_HEADS = 4
GM_WIDTH = N_GROUPS * LANES
DN_WIDTH = N_HEADS * LANES
CONV_W = 4
HIST = 8
XA_HEADS = 4
VMEM_LIMIT = 56 * 1024 * 1024

TS_MIX = 512
TS_XA = 512
TM_FFN = 512
FF_BLOCK = 1024


def _rms(x, g):
    return x * lax.rsqrt(jnp.mean(x * x, axis=-1, keepdims=True) + EPS) * g


def _gelu_tanh(x):
    c = math.sqrt(2.0 / math.pi)
    return x * (0.5 * (1.0 + jnp.tanh(c * (x + 0.044715 * (x * x * x)))))


def _sigmoid(x):
    return 1.0 / (1.0 + jnp.exp(-x))


def _silu(x):
    return x * _sigmoid(x)


def _softplus(x):
    return jnp.maximum(x, 0.0) + jnp.log1p(jnp.exp(-jnp.abs(x)))


def _dot(a, b):
    return jnp.dot(a, b, preferred_element_type=F32)


def _dot_nt(a, b):
    return lax.dot_general(a, b, (((1,), (1,)), ((), ())), preferred_element_type=F32)


def _memkv_kernel(mem_ref, g_ref, wk_ref, wv_ref, k_ref, v_ref):
    m = _rms(mem_ref[0], g_ref[...]).astype(BF16)
    k_ref[0] = _dot(m, wk_ref[...]).astype(BF16)
    v_ref[0] = _dot(m, wv_ref[...]).astype(BF16)


def _memkv(mem, g, wk, wv):
    b, nm, d = mem.shape
    full = lambda shape: pl.BlockSpec(shape, lambda i: (0,) * len(shape))
    return pl.pallas_call(
        _memkv_kernel,
        grid=(b,),
        in_specs=[pl.BlockSpec((1, nm, d), lambda i: (i, 0, 0)), full((1, d)), full((d, d)), full((d, d))],
        out_specs=[pl.BlockSpec((1, nm, d), lambda i: (i, 0, 0))] * 2,
        out_shape=[jax.ShapeDtypeStruct((b, nm, d), BF16)] * 2,
        compiler_params=pltpu.CompilerParams(dimension_semantics=("arbitrary",), vmem_limit_bytes=VMEM_LIMIT),
        name="memkv",
    )(mem, g, wk, wv)


def _mixer_kernel(x_ref, nrm_ref, wgm_ref, wqkv_ref, wgate_ref, wab_ref, lng_ref, lnb_ref, ws_ref, bs_ref,
                  conv_ref, alog_ref, dtb_ref, onorm_ref, wout_ref,
                  out_ref,
                  cbuf, state, qkv, gcol, drow, u_s, w_s, qkm_s, qd_s, kdt_s, o_s, ymix):
    ts = x_ref.shape[1]
    n_chunks = ts // CHUNK

    @pl.when(pl.program_id(1) == 0)
    def _():
        cbuf[0:HIST, :] = jnp.zeros((HIST, 3 * DN_WIDTH), F32)
        state[...] = jnp.zeros_like(state)

    x = x_ref[0]
    hn = _rms(x, nrm_ref[...]).astype(BF16)

    row = lax.broadcasted_iota(jnp.int32, (CHUNK, CHUNK), 0)
    col = lax.broadcasted_iota(jnp.int32, (CHUNK, CHUNK), 1)
    tri = row >= col
    strict = row > col
    rc = row ^ col
    blk = jnp.zeros((CHUNK, CHUNK), jnp.int32)
    size = INV_BASE
    while size < CHUNK:
        blk = jnp.where(rc >= size, size, blk)
        size *= 2

    guv = _dot(hn, wgm_ref[...])
    for g in range(N_GROUPS):
        lo, hi = g * LANES, (g + 1) * LANES
        u = _gelu_tanh(guv[:, lo:hi])
        v = _gelu_tanh(guv[:, GM_WIDTH + lo:GM_WIDTH + hi])
        mu = jnp.mean(v, axis=-1, keepdims=True)
        vc = v - mu
        var = jnp.mean(vc * vc, axis=-1, keepdims=True)
        vn = (vc * lax.rsqrt(var + EPS) * lng_ref[g:g + 1, :] + lnb_ref[g:g + 1, :]).astype(BF16)
        wtri = jnp.where(tri, ws_ref[g], 0.0).astype(BF16)
        for c in range(n_chunks):
            r0, r1 = c * CHUNK, (c + 1) * CHUNK
            mixed = _dot(wtri, vn[r0:r1]) + bs_ref[:, lo:hi]
            ymix[r0:r1, lo:hi] = (u[r0:r1] * mixed).astype(BF16)

    cbuf[HIST:HIST + ts, :] = _dot(hn, wqkv_ref[...])
    acc = cbuf[HIST:HIST + ts, :] * conv_ref[CONV_W - 1:CONV_W, :]
    for j in range(CONV_W - 1):
        back = CONV_W - 1 - j
        acc = acc + cbuf[HIST - back:HIST - back + ts, :] * conv_ref[j:j + 1, :]
    cbuf[0:HIST, :] = cbuf[ts:ts + HIST, :]
    act = _silu(acc)
    for h in range(N_HEADS):
        lo, hi = h * LANES, (h + 1) * LANES
        q = act[:, lo:hi]
        k = act[:, DN_WIDTH + lo:DN_WIDTH + hi]
        q = q * lax.rsqrt(jnp.sum(q * q, axis=-1, keepdims=True) + EPS)
        k = k * lax.rsqrt(jnp.sum(k * k, axis=-1, keepdims=True) + EPS)
        qkv[:, lo:hi] = q * (LANES ** -0.5)
        qkv[:, DN_WIDTH + lo:DN_WIDTH + hi] = k
    qkv[:, 2 * DN_WIDTH:] = act[:, 2 * DN_WIDTH:]

    ab = _dot(hn, wab_ref[...])
    beta = _sigmoid(ab)
    gdec = -jnp.exp(alog_ref[...]) * _softplus(ab + dtb_ref[...])
    pos = lax.broadcasted_iota(jnp.int32, (ts, LANES), 0) % CHUNK
    d = gdec
    shift = 1
    while shift < CHUNK:
        d = d + jnp.where(pos >= shift, pltpu.roll(d, shift, axis=0), 0.0)
        shift *= 2
    lane = lax.broadcasted_iota(jnp.int32, (ts, LANES), 1)
    gcol[...] = jnp.where(lane < N_HEADS, beta, d)
    for c in range(n_chunks):
        drow[c] = jnp.transpose(d[c * CHUNK:(c + 1) * CHUNK])

    def phase_a(c, carry):
        r0 = pl.multiple_of(c * CHUNK, CHUNK)
        gc = gcol[pl.ds(r0, CHUNK), :]
        dr = drow[c]
        for h in range(N_HEADS):
            lo, hi = h * LANES, (h + 1) * LANES
            q = qkv[pl.ds(r0, CHUNK), lo:hi]
            k = qkv[pl.ds(r0, CHUNK), DN_WIDTH + lo:DN_WIDTH + hi]
            v = qkv[pl.ds(r0, CHUNK), 2 * DN_WIDTH + lo:2 * DN_WIDTH + hi]
            b_c = gc[:, h:h + 1]
            d_c = gc[:, N_HEADS + h:N_HEADS + h + 1]
            d_r = dr[N_HEADS + h:N_HEADS + h + 1, :]
            d_last = d_c[CHUNK - 1:CHUNK, :]
            e_d = jnp.exp(d_c)
            kb = k * b_c
            vb = v * b_c
            kbd = kb * e_d
            kbf = k.astype(BF16)
            both = _dot_nt(jnp.concatenate([kb, q], axis=0).astype(BF16), kbf)
            lmask = jnp.where(tri, jnp.exp(jnp.minimum(d_c - d_r, 0.0)), 0.0)
            neg_m = jnp.where(strict, -both[:CHUNK] * lmask, 0.0)
            qkm_s[pl.ds(r0, CHUNK), lo:hi] = (both[CHUNK:] * lmask).astype(BF16)
            pw = jnp.where(blk == 0, neg_m, 0.0)
            r_acc = pw
            for _ in range(INV_BASE.bit_length() - 2):
                pwb = pw.astype(BF16)
                pw = _dot(pwb, pwb)
                r_acc = r_acc + pw + _dot(r_acc.astype(BF16), pw.astype(BF16))
            size = INV_BASE
            while size < CHUNK:
                off = jnp.where(blk == size, neg_m, 0.0).astype(BF16)
                a = off + _dot(r_acc.astype(BF16), off)
                r_acc = r_acc + a + _dot(a.astype(BF16), r_acc.astype(BF16))
                size *= 2
            rhs = jnp.concatenate([vb, kbd], axis=1)
            sol = rhs + _dot(r_acc.astype(BF16), rhs.astype(BF16))
            u_s[pl.ds(r0, CHUNK), lo:hi] = sol[:, :LANES]
            w_s[pl.ds(r0, CHUNK), lo:hi] = sol[:, LANES:].astype(BF16)
            qd_s[pl.ds(r0, CHUNK), lo:hi] = (q * e_d).astype(BF16)
            kdt_s[c, lo:hi, :] = jnp.transpose(k * jnp.exp(d_last - d_c)).astype(BF16)
        return carry

    lax.fori_loop(0, n_chunks, phase_a, 0)

    def phase_b(c, carry):
        r0 = pl.multiple_of(c * CHUNK, CHUNK)
        last = gcol[pl.ds(r0 + CHUNK - 1, 1), :]
        for h in range(N_HEADS):
            lo, hi = h * LANES, (h + 1) * LANES
            s_old = state[h]
            wq = jnp.concatenate([w_s[pl.ds(r0, CHUNK), lo:hi], qd_s[pl.ds(r0, CHUNK), lo:hi]], axis=0)
            ws_qs = _dot(wq, s_old.astype(BF16))
            v_new = (u_s[pl.ds(r0, CHUNK), lo:hi] - ws_qs[:CHUNK]).astype(BF16)
            o_s[pl.ds(r0, CHUNK), lo:hi] = ws_qs[CHUNK:] + _dot(qkm_s[pl.ds(r0, CHUNK), lo:hi], v_new)
            cd = jnp.exp(last[:, N_HEADS + h:N_HEADS + h + 1])
            state[h] = s_old * cd + _dot(kdt_s[c, lo:hi, :], v_new)
        return carry

    lax.fori_loop(0, n_chunks, phase_b, 0)

    gate = _silu(_dot(hn, wgate_ref[...]))
    for h in range(N_HEADS):
        lo, hi = h * LANES, (h + 1) * LANES
        ymix[:, GM_WIDTH + lo:GM_WIDTH + hi] = (_rms(o_s[:, lo:hi], onorm_ref[...]) * gate[:, lo:hi]).astype(BF16)

    out_ref[0] = x + _dot(ymix[...], wout_ref[...])


def _mixer(x, nrm, wgm, wqkv, wgate, wab, lng, lnb, ws, bs_full, conv, alog, dtb, onorm, wout):
    b, s, d = x.shape
    ts = TS_MIX
    full = lambda shape: pl.BlockSpec(shape, lambda i, j: (0,) * len(shape), pipeline_mode=pl.Buffered(1))
    tile = pl.BlockSpec((1, ts, d), lambda i, j: (i, j, 0))
    consts = (nrm, wgm, wqkv, wgate, wab, lng, lnb, ws, bs_full, conv, alog, dtb, onorm, wout)
    return pl.pallas_call(
        _mixer_kernel,
        grid=(b, s // ts),
        in_specs=[tile] + [full(a.shape) for a in consts],
        out_specs=tile,
        out_shape=jax.ShapeDtypeStruct(x.shape, F32),
        scratch_shapes=[
            pltpu.VMEM((HIST + ts, 3 * DN_WIDTH), F32),
            pltpu.VMEM((N_HEADS, LANES, LANES), F32),
            pltpu.VMEM((ts, 3 * DN_WIDTH), F32),
            pltpu.VMEM((ts, LANES), F32),
            pltpu.VMEM((ts // CHUNK, LANES, CHUNK), F32),
            pltpu.VMEM((ts, DN_WIDTH), F32),
            pltpu.VMEM((ts, DN_WIDTH), BF16),
            pltpu.VMEM((ts, DN_WIDTH), BF16),
            pltpu.VMEM((ts, DN_WIDTH), BF16),
            pltpu.VMEM((ts // CHUNK, DN_WIDTH, CHUNK), BF16),
            pltpu.VMEM((ts, DN_WIDTH), F32),
            pltpu.VMEM((ts, GM_WIDTH + DN_WIDTH), BF16),
        ],
        compiler_params=pltpu.CompilerParams(dimension_semantics=("arbitrary", "arbitrary"),
                                             vmem_limit_bytes=VMEM_LIMIT),
        name="mixer",
    )(x, *consts)


def _xattn_kernel(x_ref, g_ref, wq_ref, k_ref, v_ref, wo_ref, out_ref, o_s):
    x = x_ref[0]
    d = x.shape[-1]
    hd = d // XA_HEADS
    hn = _rms(x, g_ref[...]).astype(BF16)
    q = _dot(hn, wq_ref[...]).astype(BF16)
    for h in range(XA_HEADS):
        lo, hi = h * hd, (h + 1) * hd
        s = _dot_nt(q[:, lo:hi], k_ref[0, :, lo:hi]) * (hd ** -0.5)
        e = jnp.exp(s - jnp.max(s, axis=-1, keepdims=True))
        p = (e / jnp.sum(e, axis=-1, keepdims=True)).astype(BF16)
        o_s[:, lo:hi] = _dot(p, v_ref[0, :, lo:hi]).astype(BF16)
    out_ref[0] = x + _dot(o_s[...], wo_ref[...])


def _xattn(x, g, wq, k, v, wo):
    b, s, d = x.shape
    nm = k.shape[1]
    ts = TS_XA
    full = lambda shape: pl.BlockSpec(shape, lambda i, j: (0,) * len(shape))
    tile = pl.BlockSpec((1, ts, d), lambda i, j: (i, j, 0))
    kv = pl.BlockSpec((1, nm, d), lambda i, j: (i, 0, 0))
    return pl.pallas_call(
        _xattn_kernel,
        grid=(b, s // ts),
        in_specs=[tile, full((1, d)), full((d, d)), kv, kv, full((d, d))],
        out_specs=tile,
        out_shape=jax.ShapeDtypeStruct(x.shape, F32),
        scratch_shapes=[pltpu.VMEM((ts, d), BF16)],
        compiler_params=pltpu.CompilerParams(dimension_semantics=("arbitrary", "arbitrary"),
                                             vmem_limit_bytes=VMEM_LIMIT),
        name="xattn",
    )(x, g, wq, k, v, wo)


def _ffn_kernel(x_ref, g_ref, w1_ref, w2_ref, gf_ref, out_ref, *, final_norm):
    x = x_ref[...]
    hn = _rms(x, g_ref[...]).astype(BF16)
    acc = x
    for j in range(w1_ref.shape[1] // FF_BLOCK):
        lo, hi = j * FF_BLOCK, (j + 1) * FF_BLOCK
        a = jnp.maximum(_dot(hn, w1_ref[:, lo:hi]), 0.0)
        acc = acc + _dot((a * a).astype(BF16), w2_ref[lo:hi, :])
    if final_norm:
        acc = _rms(acc, gf_ref[...])
    out_ref[...] = acc


def _ffn(x2, g, w1, w2, gf, final_norm):
    t, d = x2.shape
    dff = w1.shape[1]
    tm = TM_FFN
    full = lambda shape: pl.BlockSpec(shape, lambda i: (0,) * len(shape), pipeline_mode=pl.Buffered(1))
    tile = pl.BlockSpec((tm, d), lambda i: (i, 0))
    return pl.pallas_call(
        functools.partial(_ffn_kernel, final_norm=final_norm),
        grid=(t // tm,),
        in_specs=[tile, full((1, d)), full((d, dff)), full((dff, d)), full((1, d))],
        out_specs=tile,
        out_shape=jax.ShapeDtypeStruct(x2.shape, F32),
        compiler_params=pltpu.CompilerParams(dimension_semantics=("arbitrary",), vmem_limit_bytes=VMEM_LIMIT),
        name="ffn_final" if final_norm else "ffn",
    )(x2, g, w1, w2, gf)


def _pad_lanes(a, offset):
    out = jnp.zeros((a.shape[0], LANES), a.dtype)
    return out.at[:, offset:offset + a.shape[1]].set(a)


def kernel(x, mem, norm_mix, w_in, gm_ln_g, gm_ln_b, gm_ws, gm_bs, dn_conv, dn_a_log, dn_dt_bias, dn_onorm,
           w_out, norm_xa, norm_mem, xa_wq, xa_wk, xa_wv, xa_wo, norm_ffn, ffn_w1, ffn_w2, norm_final):
    b, s, d = x.shape
    depth = w_in.shape[0]
    assert s % TS_MIX == 0 and s % TS_XA == 0 and (b * s) % TM_FFN == 0 and TS_MIX % CHUNK == 0
    assert w_in.shape[2] == 2 * GM_WIDTH + 4 * DN_WIDTH + 2 * N_HEADS
    c_qkv = 2 * GM_WIDTH
    c_gate = c_qkv + 3 * DN_WIDTH
    c_ab = c_gate + DN_WIDTH
    row = lambda a: a.reshape(1, -1)
    for l in range(depth):
        wl = w_in[l]
        wgm = wl[:, :c_qkv].astype(BF16)
        wqkv = wl[:, c_qkv:c_gate].astype(BF16)
        wgate = wl[:, c_gate:c_ab].astype(BF16)
        wab = _pad_lanes(wl[:, c_ab:], 0).astype(BF16)
        bs_full = jnp.repeat(jnp.transpose(gm_bs[l]), LANES, axis=1)
        alog = _pad_lanes(row(dn_a_log[l]), N_HEADS)
        dtb = _pad_lanes(row(dn_dt_bias[l]), N_HEADS)
        x = _mixer(x, row(norm_mix[l]), wgm, wqkv, wgate, wab, gm_ln_g[l], gm_ln_b[l], gm_ws[l], bs_full,
                   dn_conv[l], alog, dtb, row(dn_onorm[l]), w_out[l].astype(BF16))
        k, v = _memkv(mem, row(norm_mem[l]), xa_wk[l].astype(BF16), xa_wv[l].astype(BF16))
        x = _xattn(x, row(norm_xa[l]), xa_wq[l].astype(BF16), k, v, xa_wo[l].astype(BF16))
        x = _ffn(x.reshape(b * s, d), row(norm_ffn[l]), ffn_w1[l].astype(BF16), ffn_w2[l].astype(BF16),
                 row(norm_final), final_norm=(l == depth - 1)).reshape(b, s, d)
    return x
```

```python
import functools
import math

import jax
import jax.numpy as jnp
from jax import lax
from jax.experimental import pallas as pl
from jax.experimental.pallas import tpu as pltpu

F32 = jnp.float32
BF16 = jnp.bfloat16

EPS = 1e-6
LANES = 128
CHUNK = 128
INV_BASE = 8
N_GROUPS = 4
N_HEADS = 4
GM_WIDTH = N_GROUPS * LANES
DN_WIDTH = N_HEADS * LANES
CONV_W = 4
HIST = 8
XA_HEADS = 4
VMEM_LIMIT = 56 * 1024 * 1024

TS_MIX = 512
TS_XA = 512
TM_FFN = 512
FF_BLOCK = 1024


def _rms(x, g):
    return x * lax.rsqrt(jnp.mean(x * x, axis=-1, keepdims=True) + EPS) * g


def _gelu_tanh(x):
    c = math.sqrt(2.0 / math.pi)
    return x * (0.5 * (1.0 + jnp.tanh(c * (x + 0.044715 * (x * x * x)))))


def _sigmoid(x):
    return 1.0 / (1.0 + jnp.exp(-x))


def _silu(x):
    return x * _sigmoid(x)


def _softplus(x):
    return jnp.maximum(x, 0.0) + jnp.log1p(jnp.exp(-jnp.abs(x)))


def _dot(a, b):
    return jnp.dot(a, b, preferred_element_type=F32)


def _dot_nt(a, b):
    return lax.dot_general(a, b, (((1,), (1,)), ((), ())), preferred_element_type=F32)


def _memkv_kernel(mem_ref, g_ref, wk_ref, wv_ref, k_ref, v_ref):
    m = _rms(mem_ref[0], g_ref[...]).astype(BF16)
    k_ref[0] = _dot(m, wk_ref[...]).astype(BF16)
    v_ref[0] = _dot(m, wv_ref[...]).astype(BF16)


def _memkv(mem, g, wk, wv):
    b, nm, d = mem.shape
    full = lambda shape: pl.BlockSpec(shape, lambda i: (0,) * len(shape))
    return pl.pallas_call(
        _memkv_kernel,
        grid=(b,),
        in_specs=[pl.BlockSpec((1, nm, d), lambda i: (i, 0, 0)), full((1, d)), full((d, d)), full((d, d))],
        out_specs=[pl.BlockSpec((1, nm, d), lambda i: (i, 0, 0))] * 2,
        out_shape=[jax.ShapeDtypeStruct((b, nm, d), BF16)] * 2,
        compiler_params=pltpu.CompilerParams(dimension_semantics=("arbitrary",), vmem_limit_bytes=VMEM_LIMIT),
        name="memkv",
    )(mem, g, wk, wv)


def _mixer_kernel(x_ref, nrm_ref, wgm_ref, wqkv_ref, wgate_ref, wab_ref, lng_ref, lnb_ref, ws_ref, bs_ref,
                  conv_ref, alog_ref, dtb_ref, onorm_ref, wout_ref,
                  out_ref,
                  cbuf, state, qkv, gcol, drow, u_s, w_s, qkm_s, qd_s, kdt_s, o_s, ymix):
    ts = x_ref.shape[1]
    n_chunks = ts // CHUNK

    @pl.when(pl.program_id(1) == 0)
    def _():
        cbuf[0:HIST, :] = jnp.zeros((HIST, 3 * DN_WIDTH), F32)
        state[...] = jnp.zeros_like(state)

    x = x_ref[0]
    hn = _rms(x, nrm_ref[...]).astype(BF16)

    row = lax.broadcasted_iota(jnp.int32, (CHUNK, CHUNK), 0)
    col = lax.broadcasted_iota(jnp.int32, (CHUNK, CHUNK), 1)
    tri = row >= col
    strict = row > col
    rc = row ^ col
    blk = jnp.zeros((CHUNK, CHUNK), jnp.int32)
    size = INV_BASE
    while size < CHUNK:
        blk = jnp.where(rc >= size, size, blk)
        size *= 2

    guv = _dot(hn, wgm_ref[...])
    for g in range(N_GROUPS):
        lo, hi = g * LANES, (g + 1) * LANES
        u = _gelu_tanh(guv[:, lo:hi])
        v = _gelu_tanh(guv[:, GM_WIDTH + lo:GM_WIDTH + hi])
        mu = jnp.mean(v, axis=-1, keepdims=True)
        vc = v - mu
        var = jnp.mean(vc * vc, axis=-1, keepdims=True)
        vn = (vc * lax.rsqrt(var + EPS) * lng_ref[g:g + 1, :] + lnb_ref[g:g + 1, :]).astype(BF16)
        wtri = jnp.where(tri, ws_ref[g], 0.0).astype(BF16)
        for c in range(n_chunks):
            r0, r1 = c * CHUNK, (c + 1) * CHUNK
            mixed = _dot(wtri, vn[r0:r1]) + bs_ref[:, lo:hi]
            ymix[r0:r1, lo:hi] = (u[r0:r1] * mixed).astype(BF16)

    cbuf[HIST:HIST + ts, :] = _dot(hn, wqkv_ref[...])
    acc = cbuf[HIST:HIST + ts, :] * conv_ref[CONV_W - 1:CONV_W, :]
    for j in range(CONV_W - 1):
        back = CONV_W - 1 - j
        acc = acc + cbuf[HIST - back:HIST - back + ts, :] * conv_ref[j:j + 1, :]
    cbuf[0:HIST, :] = cbuf[ts:ts + HIST, :]
    act = _silu(acc)
    for h in range(N_HEADS):
        lo, hi = h * LANES, (h + 1) * LANES
        q = act[:, lo:hi]
        k = act[:, DN_WIDTH + lo:DN_WIDTH + hi]
        q = q * lax.rsqrt(jnp.sum(q * q, axis=-1, keepdims=True) + EPS)
        k = k * lax.rsqrt(jnp.sum(k * k, axis=-1, keepdims=True) + EPS)
        qkv[:, lo:hi] = q * (LANES ** -0.5)
        qkv[:, DN_WIDTH + lo:DN_WIDTH + hi] = k
    qkv[:, 2 * DN_WIDTH:] = act[:, 2 * DN_WIDTH:]

    ab = _dot(hn, wab_ref[...])
    beta = _sigmoid(ab)
    gdec = -jnp.exp(alog_ref[...]) * _softplus(ab + dtb_ref[...])
    pos = lax.broadcasted_iota(jnp.int32, (ts, LANES), 0) % CHUNK
    d = gdec
    shift = 1
    while shift < CHUNK:
        d = d + jnp.where(pos >= shift, pltpu.roll(d, shift, axis=0), 0.0)
        shift *= 2
    lane = lax.broadcasted_iota(jnp.int32, (ts, LANES), 1)
    gcol[...] = jnp.where(lane < N_HEADS, beta, d)
    for c in range(n_chunks):
        drow[c] = jnp.transpose(d[c * CHUNK:(c + 1) * CHUNK])

    def phase_a():
        units = []
        for c in range(n_chunks):
            r0 = c * CHUNK
            gc = gcol[pl.ds(r0, CHUNK), :]
            dr = drow[c]
            for h in range(N_HEADS):
                units.append(dict(c=c, r0=r0, lo=h * LANES, hi=(h + 1) * LANES,
                                  b_c=gc[:, h:h + 1],
                                  d_c=gc[:, N_HEADS + h:N_HEADS + h + 1],
                                  d_r=dr[N_HEADS + h:N_HEADS + h + 1, :]))

        for u in units:
            rows, lo, hi = pl.ds(u["r0"], CHUNK), u["lo"], u["hi"]
            q = qkv[rows, lo:hi]
            k = qkv[rows, DN_WIDTH + lo:DN_WIDTH + hi]
            kb = k * u["b_c"]
            u["both"] = _dot_nt(jnp.concatenate([kb, q], axis=0).astype(BF16), k.astype(BF16))
        for u in units:
            rows, lo, hi = pl.ds(u["r0"], CHUNK), u["lo"], u["hi"]
            lmask = jnp.where(tri, jnp.exp(jnp.minimum(u["d_c"] - u["d_r"], 0.0)), 0.0)
            both = u.pop("both")
            u["neg_m"] = jnp.where(strict, -both[:CHUNK] * lmask, 0.0)
            qkm_s[rows, lo:hi] = (both[CHUNK:] * lmask).astype(BF16)

        for u in units:
            n0 = jnp.where(blk == 0, u["neg_m"], 0.0)
            u["r"] = n0
            nb = n0.astype(BF16)
            u["y"] = _dot(nb, nb)
        n_steps = INV_BASE.bit_length() - 2
        for step in range(n_steps):
            for u in units:
                yb = u["y"].astype(BF16)
                u["r_new"] = u["r"] + u["y"] + _dot(u["r"].astype(BF16), yb)
                if step + 1 < n_steps:
                    u["y"] = _dot(yb, yb)
            for u in units:
                u["r"] = u.pop("r_new")
        size = INV_BASE
        while size < CHUNK:
            for u in units:
                off = jnp.where(blk == size, u["neg_m"], 0.0).astype(BF16)
                u["a"] = off + _dot(u["r"].astype(BF16), off)
            for u in units:
                a = u.pop("a")
                u["r"] = u["r"] + a + _dot(a.astype(BF16), u["r"].astype(BF16))
            size *= 2

        for u in units:
            rows, lo, hi = pl.ds(u["r0"], CHUNK), u["lo"], u["hi"]
            k = qkv[rows, DN_WIDTH + lo:DN_WIDTH + hi]
            v = qkv[rows, 2 * DN_WIDTH + lo:2 * DN_WIDTH + hi]
            e_d = jnp.exp(u["d_c"])
            kb = k * u["b_c"]
            rhs = jnp.concatenate([v * u["b_c"], kb * e_d], axis=1)
            sol = rhs + _dot(u["r"].astype(BF16), rhs.astype(BF16))
            u_s[rows, lo:hi] = sol[:, :LANES]
            w_s[rows, lo:hi] = sol[:, LANES:].astype(BF16)
        for u in units:
            rows, lo, hi = pl.ds(u["r0"], CHUNK), u["lo"], u["hi"]
            q = qkv[rows, lo:hi]
            k = qkv[rows, DN_WIDTH + lo:DN_WIDTH + hi]
            d_c = u["d_c"]
            d_last = d_c[CHUNK - 1:CHUNK, :]
            qd_s[rows, lo:hi] = (q * jnp.exp(d_c)).astype(BF16)
            kdt_s[u["c"], lo:hi, :] = jnp.transpose(k * jnp.exp(d_last - d_c)).astype(BF16)

    phase_a()

    def phase_b(c):
        r0 = c * CHUNK
        rows = pl.ds(r0, CHUNK)
        last = gcol[r0 + CHUNK - 1:r0 + CHUNK, :]
        heads = [dict(h=h, lo=h * LANES, hi=(h + 1) * LANES) for h in range(N_HEADS)]
        for u in heads:
            lo, hi = u["lo"], u["hi"]
            u["s_old"] = state[u["h"]]
            wq = jnp.concatenate([w_s[rows, lo:hi], qd_s[rows, lo:hi]], axis=0)
            u["ws_qs"] = _dot(wq, u["s_old"].astype(BF16))
        for u in heads:
            lo, hi = u["lo"], u["hi"]
            u["v_new"] = (u_s[rows, lo:hi] - u["ws_qs"][:CHUNK]).astype(BF16)
            cd = jnp.exp(last[:, N_HEADS + u["h"]:N_HEADS + u["h"] + 1])
            state[u["h"]] = u["s_old"] * cd + _dot(kdt_s[c, lo:hi, :], u["v_new"])
        for u in heads:
            lo, hi = u["lo"], u["hi"]
            o_s[rows, lo:hi] = u["ws_qs"][CHUNK:] + _dot(qkm_s[rows, lo:hi], u["v_new"])

    for c in range(n_chunks):
        phase_b(c)

    gate = _silu(_dot(hn, wgate_ref[...]))
    for h in range(N_HEADS):
        lo, hi = h * LANES, (h + 1) * LANES
        ymix[:, GM_WIDTH + lo:GM_WIDTH + hi] = (_rms(o_s[:, lo:hi], onorm_ref[...]) * gate[:, lo:hi]).astype(BF16)

    out_ref[0] = x + _dot(ymix[...], wout_ref[...])


def _mixer(x, nrm, wgm, wqkv, wgate, wab, lng, lnb, ws, bs_full, conv, alog, dtb, onorm, wout):
    b, s, d = x.shape
    ts = TS_MIX
    full = lambda shape: pl.BlockSpec(shape, lambda i, j: (0,) * len(shape), pipeline_mode=pl.Buffered(1))
    tile = pl.BlockSpec((1, ts, d), lambda i, j: (i, j, 0))
    consts = (nrm, wgm, wqkv, wgate, wab, lng, lnb, ws, bs_full, conv, alog, dtb, onorm, wout)
    return pl.pallas_call(
        _mixer_kernel,
        grid=(b, s // ts),
        in_specs=[tile] + [full(a.shape) for a in consts],
        out_specs=tile,
        out_shape=jax.ShapeDtypeStruct(x.shape, F32),
        scratch_shapes=[
            pltpu.VMEM((HIST + ts, 3 * DN_WIDTH), F32),
            pltpu.VMEM((N_HEADS, LANES, LANES), F32),
            pltpu.VMEM((ts, 3 * DN_WIDTH), F32),
            pltpu.VMEM((ts, LANES), F32),
            pltpu.VMEM((ts // CHUNK, LANES, CHUNK), F32),
            pltpu.VMEM((ts, DN_WIDTH), F32),
            pltpu.VMEM((ts, DN_WIDTH), BF16),
            pltpu.VMEM((ts, DN_WIDTH), BF16),
            pltpu.VMEM((ts, DN_WIDTH), BF16),
            pltpu.VMEM((ts // CHUNK, DN_WIDTH, CHUNK), BF16),
            pltpu.VMEM((ts, DN_WIDTH), F32),
            pltpu.VMEM((ts, GM_WIDTH + DN_WIDTH), BF16),
        ],
        compiler_params=pltpu.CompilerParams(dimension_semantics=("arbitrary", "arbitrary"),
                                             vmem_limit_bytes=VMEM_LIMIT),
        name="mixer",
    )(x, *consts)


def _xattn_kernel(x_ref, g_ref, wq_ref, k_ref, v_ref, wo_ref, out_ref, o_s):
    x = x_ref[0]
    d = x.shape[-1]
    hd = d // XA_HEADS
    hn = _rms(x, g_ref[...]).astype(BF16)
    q = _dot(hn, wq_ref[...]).astype(BF16)
    for h in range(XA_HEADS):
        lo, hi = h * hd, (h + 1) * hd
        s = _dot_nt(q[:, lo:hi], k_ref[0, :, lo:hi]) * (hd ** -0.5)
        e = jnp.exp(s - jnp.max(s, axis=-1, keepdims=True))
        p = (e / jnp.sum(e, axis=-1, keepdims=True)).astype(BF16)
        o_s[:, lo:hi] = _dot(p, v_ref[0, :, lo:hi]).astype(BF16)
    out_ref[0] = x + _dot(o_s[...], wo_ref[...])


def _xattn(x, g, wq, k, v, wo):
    b, s, d = x.shape
    nm = k.shape[1]
    ts = TS_XA
    full = lambda shape: pl.BlockSpec(shape, lambda i, j: (0,) * len(shape))
    tile = pl.BlockSpec((1, ts, d), lambda i, j: (i, j, 0))
    kv = pl.BlockSpec((1, nm, d), lambda i, j: (i, 0, 0))
    return pl.pallas_call(
        _xattn_kernel,
        grid=(b, s // ts),
        in_specs=[tile, full((1, d)), full((d, d)), kv, kv, full((d, d))],
        out_specs=tile,
        out_shape=jax.ShapeDtypeStruct(x.shape, F32),
        scratch_shapes=[pltpu.VMEM((ts, d), BF16)],
        compiler_params=pltpu.CompilerParams(dimension_semantics=("arbitrary", "arbitrary"),
                                             vmem_limit_bytes=VMEM_LIMIT),
        name="xattn",
    )(x, g, wq, k, v, wo)


def _ffn_kernel(x_ref, g_ref, w1_ref, w2_ref, gf_ref, out_ref, *, final_norm):
    x = x_ref[...]
    hn = _rms(x, g_ref[...]).astype(BF16)
    acc = x
    for j in range(w1_ref.shape[1] // FF_BLOCK):
        lo, hi = j * FF_BLOCK, (j + 1) * FF_BLOCK
        a = jnp.maximum(_dot(hn, w1_ref[:, lo:hi]), 0.0)
        acc = acc + _dot((a * a).astype(BF16), w2_ref[lo:hi, :])
    if final_norm:
        acc = _rms(acc, gf_ref[...])
    out_ref[...] = acc


def _ffn(x2, g, w1, w2, gf, final_norm):
    t, d = x2.shape
    dff = w1.shape[1]
    tm = TM_FFN
    full = lambda shape: pl.BlockSpec(shape, lambda i: (0,) * len(shape), pipeline_mode=pl.Buffered(1))
    tile = pl.BlockSpec((tm, d), lambda i: (i, 0))
    return pl.pallas_call(
        functools.partial(_ffn_kernel, final_norm=final_norm),
        grid=(t // tm,),
        in_specs=[tile, full((1, d)), full((d, dff)), full((dff, d)), full((1, d))],
        out_specs=tile,
        out_shape=jax.ShapeDtypeStruct(x2.shape, F32),
        compiler_params=pltpu.CompilerParams(dimension_semantics=("arbitrary",), vmem_limit_bytes=VMEM_LIMIT),
        name="ffn_final" if final_norm else "ffn",
    )(x2, g, w1, w2, gf)


def _pad_lanes(a, offset):
    out = jnp.zeros((a.shape[0], LANES), a.dtype)
    return out.at[:, offset:offset + a.shape[1]].set(a)


def kernel(x, mem, norm_mix, w_in, gm_ln_g, gm_ln_b, gm_ws, gm_bs, dn_conv, dn_a_log, dn_dt_bias, dn_onorm,
           w_out, norm_xa, norm_mem, xa_wq, xa_wk, xa_wv, xa_wo, norm_ffn, ffn_w1, ffn_w2, norm_final):
    b, s, d = x.shape
    depth = w_in.shape[0]
    assert s % TS_MIX == 0 and s % TS_XA == 0 and (b * s) % TM_FFN == 0 and TS_MIX % CHUNK == 0
    assert w_in.shape[2] == 2 * GM_WIDTH + 4 * DN_WIDTH + 2 * N_HEADS
    c_qkv = 2 * GM_WIDTH
    c_gate = c_qkv + 3 * DN_WIDTH
    c_ab = c_gate + DN_WIDTH
    row = lambda a: a.reshape(1, -1)
    for l in range(depth):
        wl = w_in[l]
        wgm = wl[:, :c_qkv].astype(BF16)
        wqkv = wl[:, c_qkv:c_gate].astype(BF16)
        wgate = wl[:, c_gate:c_ab].astype(BF16)
        wab = _pad_lanes(wl[:, c_ab:], 0).astype(BF16)
        bs_full = jnp.repeat(jnp.transpose(gm_bs[l]), LANES, axis=1)
        alog = _pad_lanes(row(dn_a_log[l]), N_HEADS)
        dtb = _pad_lanes(row(dn_dt_bias[l]), N_HEADS)
        x = _mixer(x, row(norm_mix[l]), wgm, wqkv, wgate, wab, gm_ln_g[l], gm_ln_b[l], gm_ws[l], bs_full,
                   dn_conv[l], alog, dtb, row(dn_onorm[l]), w_out[l].astype(BF16))
        k, v = _memkv(mem, row(norm_mem[l]), xa_wk[l].astype(BF16), xa_wv[l].astype(BF16))
        x = _xattn(x, row(norm_xa[l]), xa_wq[l].astype(BF16), k, v, xa_wo[l].astype(BF16))
        x = _ffn(x.reshape(b * s, d), row(norm_ffn[l]), ffn_w1[l].astype(BF16), ffn_w2[l].astype(BF16),
                 row(norm_final), final_norm=(l == depth - 1)).reshape(b, s, d)
    return x
```

```python
import functools
import math

import jax
import jax.numpy as jnp
from jax import lax
from jax.experimental import pallas as pl
from jax.experimental.pallas import tpu as pltpu

F32 = jnp.float32
BF16 = jnp.bfloat16

EPS = 1e-6
LANES = 128
CHUNK = 128
INV_BASE = 8
N_GROUPS = 4
N_HEADS = 4
GM_WIDTH = N_GROUPS * LANES
DN_WIDTH = N_HEADS * LANES
CONV_W = 4
HIST = 8
XA_HEADS = 4
VMEM_LIMIT = 56 * 1024 * 1024

TS_MIX = 512
TS_XA = 512
TM_FFN = 512
FF_BLOCK = 1024
PROJ_BLOCK = 256

C_QKV = 2 * GM_WIDTH
C_GATE = C_QKV + 3 * DN_WIDTH
C_AB = C_GATE + DN_WIDTH
D_IN_PADDED = C_AB + LANES


def _layer_spec(a, l):
    zeros = (0,) * (a.ndim - 1)
    return pl.BlockSpec((None,) + a.shape[1:], lambda *_: (l,) + zeros, pipeline_mode=pl.Buffered(1))


def _rms(x, g):
    return x * lax.rsqrt(jnp.mean(x * x, axis=-1, keepdims=True) + EPS) * g


def _gelu_tanh(x):
    c = math.sqrt(2.0 / math.pi)
    return x * (0.5 * (1.0 + jnp.tanh(c * (x + 0.044715 * (x * x * x)))))


def _sigmoid(x):
    return 1.0 / (1.0 + jnp.exp(-x))


def _silu(x):
    return x * _sigmoid(x)


def _softplus(x):
    return jnp.maximum(x, 0.0) + jnp.log1p(jnp.exp(-jnp.abs(x)))


def _dot(a, b):
    return jnp.dot(a, b, preferred_element_type=F32)


def _dot_nt(a, b):
    return lax.dot_general(a, b, (((1,), (1,)), ((), ())), preferred_element_type=F32)


def _memkv_kernel(mem_ref, g_ref, wk_ref, wv_ref, k_ref, v_ref):
    m = _rms(mem_ref[0], g_ref[...]).astype(BF16)
    k_ref[0] = _dot(m, wk_ref[...]).astype(BF16)
    v_ref[0] = _dot(m, wv_ref[...]).astype(BF16)


def _memkv(mem, l, g, wk, wv):
    b, nm, d = mem.shape
    return pl.pallas_call(
        _memkv_kernel,
        grid=(b,),
        in_specs=[pl.BlockSpec((1, nm, d), lambda i: (i, 0, 0))] + [_layer_spec(a, l) for a in (g, wk, wv)],
        out_specs=[pl.BlockSpec((1, nm, d), lambda i: (i, 0, 0))] * 2,
        out_shape=[jax.ShapeDtypeStruct((b, nm, d), BF16)] * 2,
        compiler_params=pltpu.CompilerParams(dimension_semantics=("arbitrary",), vmem_limit_bytes=VMEM_LIMIT),
        name="memkv",
    )(mem, g, wk, wv)


def _mixer_kernel(x_ref, nrm_ref, win_ref, lng_ref, lnb_ref, ws_ref, bs_ref,
                  conv_ref, alog_ref, dtb_ref, onorm_ref, wout_ref,
                  out_ref,
                  cbuf, state, qkv, gcol, drow, u_s, w_s, qkm_s, qd_s, kdt_s, o_s, ymix, guv_s, gate_s):
    ts = x_ref.shape[1]
    n_chunks = ts // CHUNK
    wgm_ref = win_ref.at[:, 0:C_QKV]
    wqkv_ref = win_ref.at[:, C_QKV:C_GATE]
    wgate_ref = win_ref.at[:, C_GATE:C_AB]
    wab_ref = win_ref.at[:, C_AB:D_IN_PADDED]

    @pl.when(pl.program_id(1) == 0)
    def _():
        cbuf[0:HIST, :] = jnp.zeros((HIST, 3 * DN_WIDTH), F32)
        state[...] = jnp.zeros_like(state)

    x = x_ref[0]
    hn = _rms(x, nrm_ref[...]).astype(BF16)

    row = lax.broadcasted_iota(jnp.int32, (CHUNK, CHUNK), 0)
    col = lax.broadcasted_iota(jnp.int32, (CHUNK, CHUNK), 1)
    tri = row >= col
    strict = row > col
    eye = jnp.where(row == col, 1.0, 0.0)
    rc = row ^ col
    blk = jnp.zeros((CHUNK, CHUNK), jnp.int32)
    size = INV_BASE
    while size < CHUNK:
        blk = jnp.where(rc >= size, size, blk)
        size *= 2

    cbuf[HIST:HIST + ts, :] = _dot(hn, wqkv_ref[...])
    ab = _dot(hn, wab_ref[...])
    beta = _sigmoid(ab)
    gdec = -jnp.exp(alog_ref[...]) * _softplus(ab + dtb_ref[...])
    pos = lax.broadcasted_iota(jnp.int32, (ts, LANES), 0) % CHUNK
    d = gdec
    shift = 1
    while shift < CHUNK:
        d = d + jnp.where(pos >= shift, pltpu.roll(d, shift, axis=0), 0.0)
        shift *= 2
    lane = lax.broadcasted_iota(jnp.int32, (ts, LANES), 1)
    gcol[...] = jnp.where(lane < N_HEADS, beta, d)
    for c in range(n_chunks):
        drow[c] = jnp.transpose(d[c * CHUNK:(c + 1) * CHUNK])

    def conv_block(lo, hi):
        xe = cbuf[:, lo:hi]
        x1 = pltpu.roll(xe, 1, axis=0)
        z = xe * conv_ref[1:2, lo:hi] + x1 * conv_ref[0:1, lo:hi]
        y = xe * conv_ref[3:4, lo:hi] + x1 * conv_ref[2:3, lo:hi] + pltpu.roll(z, 2, axis=0)
        return _silu(y[HIST:])

    def unit_norm(a):
        return a * lax.rsqrt(jnp.sum(a * a, axis=-1, keepdims=True) + EPS)

    def proj_block(dst, w_ref, t):
        dst[:, t * PROJ_BLOCK:(t + 1) * PROJ_BLOCK] = _dot(hn, w_ref[:, t * PROJ_BLOCK:(t + 1) * PROJ_BLOCK])

    mxu_pieces = [functools.partial(proj_block, guv_s, wgm_ref, t) for t in range(2 * GM_WIDTH // PROJ_BLOCK)]
    mxu_pieces += [functools.partial(proj_block, gate_s, wgate_ref, t) for t in range(DN_WIDTH // PROJ_BLOCK)]
    n_blocks = 3 * DN_WIDTH // PROJ_BLOCK
    for p in range(n_blocks):
        lo = p * PROJ_BLOCK
        act = conv_block(lo, lo + PROJ_BLOCK)
        for s0 in range(0, PROJ_BLOCK, LANES):
            a = act[:, s0:s0 + LANES]
            if lo < DN_WIDTH:
                a = unit_norm(a) * (LANES ** -0.5)
            elif lo < 2 * DN_WIDTH:
                a = unit_norm(a)
            qkv[:, lo + s0:lo + s0 + LANES] = a
        if p < len(mxu_pieces):
            mxu_pieces[p]()
    for piece in mxu_pieces[n_blocks:]:
        piece()
    cbuf[0:HIST, :] = cbuf[ts:ts + HIST, :]

    def phase_a():
        units = []
        for c in range(n_chunks):
            r0 = c * CHUNK
            gc = gcol[pl.ds(r0, CHUNK), :]
            dr = drow[c]
            for h in range(N_HEADS):
                units.append(dict(c=c, r0=r0, lo=h * LANES, hi=(h + 1) * LANES,
                                  b_c=gc[:, h:h + 1],
                                  d_c=gc[:, N_HEADS + h:N_HEADS + h + 1],
                                  d_r=dr[N_HEADS + h:N_HEADS + h + 1, :]))

        for u in units:
            rows, lo, hi = pl.ds(u["r0"], CHUNK), u["lo"], u["hi"]
            q = qkv[rows, lo:hi]
            k = qkv[rows, DN_WIDTH + lo:DN_WIDTH + hi]
            kb = k * u["b_c"]
            u["both"] = _dot_nt(jnp.concatenate([kb, q], axis=0).astype(BF16), k.astype(BF16))
        for u in units:
            rows, lo, hi = pl.ds(u["r0"], CHUNK), u["lo"], u["hi"]
            lmask = jnp.where(tri, jnp.exp(jnp.minimum(u["d_c"] - u["d_r"], 0.0)), 0.0)
            both = u.pop("both")
            u["neg_m"] = jnp.where(strict, -both[:CHUNK] * lmask, 0.0)
            qkm_s[rows, lo:hi] = (both[CHUNK:] * lmask).astype(BF16)

        for u in units:
            n0 = jnp.where(blk == 0, u["neg_m"], 0.0)
            u["t"] = n0 + eye
            nb = n0.astype(BF16)
            u["y"] = _dot(nb, nb).astype(BF16)
        n_steps = INV_BASE.bit_length() - 2
        for step in range(n_steps):
            for u in units:
                u["t_new"] = u["t"] + _dot(u["t"].astype(BF16), u["y"])
                if step + 1 < n_steps:
                    u["y"] = _dot(u["y"], u["y"]).astype(BF16)
            for u in units:
                u["t"] = u.pop("t_new")
        size = INV_BASE
        while size < CHUNK:
            for u in units:
                off = jnp.where(blk == size, u["neg_m"], 0.0).astype(BF16)
                u["tb"] = u["t"].astype(BF16)
                u["a"] = _dot(u["tb"], off).astype(BF16)
            for u in units:
                u["t"] = u["t"] + _dot(u.pop("a"), u.pop("tb"))
            size *= 2

        for u in units:
            rows, lo, hi = pl.ds(u["r0"], CHUNK), u["lo"], u["hi"]
            k = qkv[rows, DN_WIDTH + lo:DN_WIDTH + hi]
            v = qkv[rows, 2 * DN_WIDTH + lo:2 * DN_WIDTH + hi]
            e_d = jnp.exp(u["d_c"])
            kb = k * u["b_c"]
            rhs = jnp.concatenate([v * u["b_c"], kb * e_d], axis=1)
            sol = _dot(u["t"].astype(BF16), rhs.astype(BF16))
            u_s[rows, lo:hi] = sol[:, :LANES]
            w_s[rows, lo:hi] = sol[:, LANES:].astype(BF16)
        for u in units:
            rows, lo, hi = pl.ds(u["r0"], CHUNK), u["lo"], u["hi"]
            q = qkv[rows, lo:hi]
            k = qkv[rows, DN_WIDTH + lo:DN_WIDTH + hi]
            d_c = u["d_c"]
            d_last = d_c[CHUNK - 1:CHUNK, :]
            qd_s[rows, lo:hi] = (q * jnp.exp(d_c)).astype(BF16)
            kdt_s[u["c"], lo:hi, :] = jnp.transpose(k * jnp.exp(d_last - d_c)).astype(BF16)

    phase_a()

    def phase_b(c):
        r0 = c * CHUNK
        rows = pl.ds(r0, CHUNK)
        last = gcol[r0 + CHUNK - 1:r0 + CHUNK, :]
        heads = [dict(h=h, lo=h * LANES, hi=(h + 1) * LANES) for h in range(N_HEADS)]
        for u in heads:
            lo, hi = u["lo"], u["hi"]
            u["s_old"] = state[u["h"]]
            wq = jnp.concatenate([w_s[rows, lo:hi], qd_s[rows, lo:hi]], axis=0)
            u["ws_qs"] = _dot(wq, u["s_old"].astype(BF16))
        for u in heads:
            lo, hi = u["lo"], u["hi"]
            u["v_new"] = (u_s[rows, lo:hi] - u["ws_qs"][:CHUNK]).astype(BF16)
            cd = jnp.exp(last[:, N_HEADS + u["h"]:N_HEADS + u["h"] + 1])
            state[u["h"]] = u["s_old"] * cd + _dot(kdt_s[c, lo:hi, :], u["v_new"])
        for u in heads:
            lo, hi = u["lo"], u["hi"]
            o_s[rows, lo:hi] = u["ws_qs"][CHUNK:] + _dot(qkm_s[rows, lo:hi], u["v_new"])

    def spatial_gating(g):
        lo, hi = g * LANES, (g + 1) * LANES
        u = _gelu_tanh(guv_s[:, lo:hi])
        v = _gelu_tanh(guv_s[:, GM_WIDTH + lo:GM_WIDTH + hi])
        mu = jnp.mean(v, axis=-1, keepdims=True)
        vc = v - mu
        var = jnp.mean(vc * vc, axis=-1, keepdims=True)
        vn = (vc * lax.rsqrt(var + EPS) * lng_ref[g:g + 1, :] + lnb_ref[g:g + 1, :]).astype(BF16)
        wtri = jnp.where(tri, ws_ref[g], 0.0).astype(BF16)
        for c in range(n_chunks):
            r0, r1 = c * CHUNK, (c + 1) * CHUNK
            mixed = _dot(wtri, vn[r0:r1]) + bs_ref[:, lo:hi]
            ymix[r0:r1, lo:hi] = (u[r0:r1] * mixed).astype(BF16)

    for i in range(max(n_chunks, N_GROUPS)):
        if i < n_chunks:
            phase_b(i)
        if i < N_GROUPS:
            spatial_gating(i)

    for h in range(N_HEADS):
        lo, hi = h * LANES, (h + 1) * LANES
        ymix[:, GM_WIDTH + lo:GM_WIDTH + hi] = (_rms(o_s[:, lo:hi], onorm_ref[...])
                                                * _silu(gate_s[:, lo:hi])).astype(BF16)

    out_ref[0] = x + _dot(ymix[...], wout_ref[...])


def _mixer(x, l, params):
    b, s, d = x.shape
    ts = TS_MIX
    tile = pl.BlockSpec((1, ts, d), lambda i, j: (i, j, 0))
    consts = params
    return pl.pallas_call(
        _mixer_kernel,
        grid=(b, s // ts),
        in_specs=[tile] + [_layer_spec(a, l) for a in consts],
        out_specs=tile,
        out_shape=jax.ShapeDtypeStruct(x.shape, F32),
        scratch_shapes=[
            pltpu.VMEM((HIST + ts, 3 * DN_WIDTH), F32),
            pltpu.VMEM((N_HEADS, LANES, LANES), F32),
            pltpu.VMEM((ts, 3 * DN_WIDTH), F32),
            pltpu.VMEM((ts, LANES), F32),
            pltpu.VMEM((ts // CHUNK, LANES, CHUNK), F32),
            pltpu.VMEM((ts, DN_WIDTH), F32),
            pltpu.VMEM((ts, DN_WIDTH), BF16),
            pltpu.VMEM((ts, DN_WIDTH), BF16),
            pltpu.VMEM((ts, DN_WIDTH), BF16),
            pltpu.VMEM((ts // CHUNK, DN_WIDTH, CHUNK), BF16),
            pltpu.VMEM((ts, DN_WIDTH), F32),
            pltpu.VMEM((ts, GM_WIDTH + DN_WIDTH), BF16),
            pltpu.VMEM((ts, 2 * GM_WIDTH), F32),
            pltpu.VMEM((ts, DN_WIDTH), F32),
        ],
        compiler_params=pltpu.CompilerParams(dimension_semantics=("arbitrary", "arbitrary"),
                                             vmem_limit_bytes=VMEM_LIMIT),
        name="mixer",
    )(x, *consts)


def _xattn_kernel(x_ref, g_ref, wq_ref, k_ref, v_ref, wo_ref, out_ref, o_s):
    x = x_ref[0]
    d = x.shape[-1]
    hd = d // XA_HEADS
    hn = _rms(x, g_ref[...]).astype(BF16)
    q = _dot(hn, wq_ref[...]).astype(BF16)
    for h in range(XA_HEADS):
        lo, hi = h * hd, (h + 1) * hd
        s = _dot_nt(q[:, lo:hi], k_ref[0, :, lo:hi]) * (hd ** -0.5)
        e = jnp.exp(s - jnp.max(s, axis=-1, keepdims=True))
        p = (e / jnp.sum(e, axis=-1, keepdims=True)).astype(BF16)
        o_s[:, lo:hi] = _dot(p, v_ref[0, :, lo:hi]).astype(BF16)
    out_ref[0] = x + _dot(o_s[...], wo_ref[...])


def _xattn(x, l, g, wq, k, v, wo):
    b, s, d = x.shape
    nm = k.shape[1]
    ts = TS_XA
    tile = pl.BlockSpec((1, ts, d), lambda i, j: (i, j, 0))
    kv = pl.BlockSpec((1, nm, d), lambda i, j: (i, 0, 0))
    return pl.pallas_call(
        _xattn_kernel,
        grid=(b, s // ts),
        in_specs=[tile, _layer_spec(g, l), _layer_spec(wq, l), kv, kv, _layer_spec(wo, l)],
        out_specs=tile,
        out_shape=jax.ShapeDtypeStruct(x.shape, F32),
        scratch_shapes=[pltpu.VMEM((ts, d), BF16)],
        compiler_params=pltpu.CompilerParams(dimension_semantics=("arbitrary", "arbitrary"),
                                             vmem_limit_bytes=VMEM_LIMIT),
        name="xattn",
    )(x, g, wq, k, v, wo)


def _ffn_kernel(x_ref, g_ref, w1_ref, w2_ref, gf_ref, out_ref, *, final_norm):
    x = x_ref[...]
    hn = _rms(x, g_ref[...]).astype(BF16)
    acc = x
    for j in range(w1_ref.shape[1] // FF_BLOCK):
        lo, hi = j * FF_BLOCK, (j + 1) * FF_BLOCK
        a = jnp.maximum(_dot(hn, w1_ref[:, lo:hi]), 0.0)
        acc = acc + _dot((a * a).astype(BF16), w2_ref[lo:hi, :])
    if final_norm:
        acc = _rms(acc, gf_ref[...])
    out_ref[...] = acc


def _ffn(x2, l, g, w1, w2, gf, final_norm):
    t, d = x2.shape
    tm = TM_FFN
    tile = pl.BlockSpec((tm, d), lambda i: (i, 0))
    return pl.pallas_call(
        functools.partial(_ffn_kernel, final_norm=final_norm),
        grid=(t // tm,),
        in_specs=[tile, _layer_spec(g, l), _layer_spec(w1, l), _layer_spec(w2, l), _layer_spec(gf, 0)],
        out_specs=tile,
        out_shape=jax.ShapeDtypeStruct(x2.shape, F32),
        compiler_params=pltpu.CompilerParams(dimension_semantics=("arbitrary",), vmem_limit_bytes=VMEM_LIMIT),
        name="ffn_final" if final_norm else "ffn",
    )(x2, g, w1, w2, gf)


def kernel(x, mem, norm_mix, w_in, gm_ln_g, gm_ln_b, gm_ws, gm_bs, dn_conv, dn_a_log, dn_dt_bias, dn_onorm,
           w_out, norm_xa, norm_mem, xa_wq, xa_wk, xa_wv, xa_wo, norm_ffn, ffn_w1, ffn_w2, norm_final):
    b, s, d = x.shape
    depth = w_in.shape[0]
    assert s % TS_MIX == 0 and s % TS_XA == 0 and (b * s) % TM_FFN == 0 and TS_MIX % CHUNK == 0
    assert w_in.shape[2] == C_AB + 2 * N_HEADS
    rows = lambda a: a.reshape(a.shape[0], 1, -1)
    w_in_p = jnp.pad(w_in, ((0, 0), (0, 0), (0, D_IN_PADDED - w_in.shape[2]))).astype(BF16)
    bs_full = jnp.repeat(jnp.swapaxes(gm_bs, 1, 2), LANES, axis=2)
    pad_heads = lambda a: jnp.pad(rows(a), ((0, 0), (0, 0), (N_HEADS, LANES - 2 * N_HEADS)))
    mixer_params = (rows(norm_mix), w_in_p, gm_ln_g, gm_ln_b, gm_ws, bs_full, dn_conv, pad_heads(dn_a_log),
                    pad_heads(dn_dt_bias), rows(dn_onorm), w_out.astype(BF16))
    wq, wk, wv, wo = (w.astype(BF16) for w in (xa_wq, xa_wk, xa_wv, xa_wo))
    w1, w2 = ffn_w1.astype(BF16), ffn_w2.astype(BF16)
    g_xa, g_mem, g_ffn, g_final = rows(norm_xa), rows(norm_mem), rows(norm_ffn), norm_final.reshape(1, 1, -1)
    for l in range(depth):
        x = _mixer(x, l, mixer_params)
        k, v = _memkv(mem, l, g_mem, wk, wv)
        x = _xattn(x, l, g_xa, wq, k, v, wo)
        x = _ffn(x.reshape(b * s, d), l, g_ffn, w1, w2, g_final, final_norm=(l == depth - 1)).reshape(b, s, d)
    return x
```

```python
import functools
import math

import jax
import jax.numpy as jnp
from jax import lax
from jax.experimental import pallas as pl
from jax.experimental.pallas import tpu as pltpu

F32 = jnp.float32
BF16 = jnp.bfloat16

EPS = 1e-6
LANES = 128
CHUNK = 128
INV_BASE = 8
N_GROUPS = 4
N_HEADS = 4
GM_WIDTH = N_GROUPS * LANES
DN_WIDTH = N_HEADS * LANES
CONV_W = 4
HIST = 8
XA_HEADS = 4
VMEM_LIMIT = 56 * 1024 * 1024

TS_MIX = 512
TS_XA = 512
TM_FFN = 512
FF_BLOCK = 1024
PROJ_BLOCK = 256

C_QKV = 2 * GM_WIDTH
C_GATE = C_QKV + 3 * DN_WIDTH
C_AB = C_GATE + DN_WIDTH
D_IN_PADDED = C_AB + LANES


def _layer_spec(a, l):
    zeros = (0,) * (a.ndim - 1)
    return pl.BlockSpec((None,) + a.shape[1:], lambda *_: (l,) + zeros, pipeline_mode=pl.Buffered(1))


def _rms(x, g):
    return x * lax.rsqrt(jnp.mean(x * x, axis=-1, keepdims=True) + EPS) * g


def _gelu_tanh(x):
    c = math.sqrt(2.0 / math.pi)
    return x * (0.5 * (1.0 + jnp.tanh(c * (x + 0.044715 * (x * x * x)))))


def _sigmoid(x):
    return 1.0 / (1.0 + jnp.exp(-x))


def _silu(x):
    return x * _sigmoid(x)


def _softplus(x):
    return jnp.maximum(x, 0.0) + jnp.log1p(jnp.exp(-jnp.abs(x)))


def _dot(a, b):
    return jnp.dot(a, b, preferred_element_type=F32)


def _dot_nt(a, b):
    return lax.dot_general(a, b, (((1,), (1,)), ((), ())), preferred_element_type=F32)


def _memkv_kernel(mem_ref, g_ref, wk_ref, wv_ref, k_ref, v_ref):
    m = _rms(mem_ref[0], g_ref[...]).astype(BF16)
    k_ref[0] = _dot(m, wk_ref[...]).astype(BF16)
    v_ref[0] = _dot(m, wv_ref[...]).astype(BF16)


def _memkv(mem, l, g, wk, wv):
    b, nm, d = mem.shape
    return pl.pallas_call(
        _memkv_kernel,
        grid=(b,),
        in_specs=[pl.BlockSpec((1, nm, d), lambda i: (i, 0, 0))] + [_layer_spec(a, l) for a in (g, wk, wv)],
        out_specs=[pl.BlockSpec((1, nm, d), lambda i: (i, 0, 0))] * 2,
        out_shape=[jax.ShapeDtypeStruct((b, nm, d), BF16)] * 2,
        compiler_params=pltpu.CompilerParams(dimension_semantics=("arbitrary",), vmem_limit_bytes=VMEM_LIMIT),
        name="memkv",
    )(mem, g, wk, wv)


def _mixer_kernel(x_ref, nrm_ref, win_ref, lng_ref, lnb_ref, ws_ref, bs_ref,
                  conv_ref, alog_ref, dtb_ref, onorm_ref, wout_ref,
                  out_ref,
                  cbuf, state, qkv, gcol, drow, u_s, w_s, qkm_s, qd_s, kdt_s, o_s, ymix, guv_s, gate_s):
    ts = x_ref.shape[1]
    n_chunks = ts // CHUNK
    wgm_ref = win_ref.at[:, 0:C_QKV]
    wqkv_ref = win_ref.at[:, C_QKV:C_GATE]
    wgate_ref = win_ref.at[:, C_GATE:C_AB]
    wab_ref = win_ref.at[:, C_AB:D_IN_PADDED]

    @pl.when(pl.program_id(1) == 0)
    def _():
        cbuf[0:HIST, :] = jnp.zeros((HIST, 3 * DN_WIDTH), F32)
        state[...] = jnp.zeros_like(state)

    x = x_ref[0]
    hn = _rms(x, nrm_ref[...]).astype(BF16)

    row = lax.broadcasted_iota(jnp.int32, (CHUNK, CHUNK), 0)
    col = lax.broadcasted_iota(jnp.int32, (CHUNK, CHUNK), 1)
    tri = row >= col
    strict = row > col
    eye = jnp.where(row == col, 1.0, 0.0)
    rc = row ^ col
    blk = jnp.zeros((CHUNK, CHUNK), jnp.int32)
    size = INV_BASE
    while size < CHUNK:
        blk = jnp.where(rc >= size, size, blk)
        size *= 2

    cbuf[HIST:HIST + ts, :] = _dot(hn, wqkv_ref[...])
    ab = _dot(hn, wab_ref[...])
    beta = _sigmoid(ab)
    gdec = -jnp.exp(alog_ref[...]) * _softplus(ab + dtb_ref[...])
    pos = lax.broadcasted_iota(jnp.int32, (ts, LANES), 0) % CHUNK
    d = gdec
    shift = 1
    while shift < CHUNK:
        d = d + jnp.where(pos >= shift, pltpu.roll(d, shift, axis=0), 0.0)
        shift *= 2
    lane = lax.broadcasted_iota(jnp.int32, (ts, LANES), 1)
    gcol[...] = jnp.where(lane < N_HEADS, beta, d)
    for c in range(n_chunks):
        drow[c] = jnp.transpose(d[c * CHUNK:(c + 1) * CHUNK])

    def conv_block(lo, hi):
        xe = cbuf[:, lo:hi]
        x1 = pltpu.roll(xe, 1, axis=0)
        z = xe * conv_ref[1:2, lo:hi] + x1 * conv_ref[0:1, lo:hi]
        y = xe * conv_ref[3:4, lo:hi] + x1 * conv_ref[2:3, lo:hi] + pltpu.roll(z, 2, axis=0)
        return _silu(y[HIST:])

    def unit_norm(a):
        return a * lax.rsqrt(jnp.sum(a * a, axis=-1, keepdims=True) + EPS)

    def proj_block(dst, w_ref, t):
        dst[:, t * PROJ_BLOCK:(t + 1) * PROJ_BLOCK] = _dot(hn, w_ref[:, t * PROJ_BLOCK:(t + 1) * PROJ_BLOCK])

    mxu_pieces = [functools.partial(proj_block, guv_s, wgm_ref, t) for t in range(2 * GM_WIDTH // PROJ_BLOCK)]
    mxu_pieces += [functools.partial(proj_block, gate_s, wgate_ref, t) for t in range(DN_WIDTH // PROJ_BLOCK)]
    n_blocks = 3 * DN_WIDTH // PROJ_BLOCK
    for p in range(n_blocks):
        lo = p * PROJ_BLOCK
        act = conv_block(lo, lo + PROJ_BLOCK)
        for s0 in range(0, PROJ_BLOCK, LANES):
            a = act[:, s0:s0 + LANES]
            if lo < DN_WIDTH:
                a = unit_norm(a) * (LANES ** -0.5)
            elif lo < 2 * DN_WIDTH:
                a = unit_norm(a)
            qkv[:, lo + s0:lo + s0 + LANES] = a
        if p < len(mxu_pieces):
            mxu_pieces[p]()
    for piece in mxu_pieces[n_blocks:]:
        piece()
    cbuf[0:HIST, :] = cbuf[ts:ts + HIST, :]

    def phase_a():
        units = []
        for c in range(n_chunks):
            r0 = c * CHUNK
            gc = gcol[pl.ds(r0, CHUNK), :]
            dr = drow[c]
            for h in range(N_HEADS):
                units.append(dict(c=c, r0=r0, lo=h * LANES, hi=(h + 1) * LANES,
                                  b_c=gc[:, h:h + 1],
                                  d_c=gc[:, N_HEADS + h:N_HEADS + h + 1],
                                  d_r=dr[N_HEADS + h:N_HEADS + h + 1, :]))

        for u in units:
            rows, lo, hi = pl.ds(u["r0"], CHUNK), u["lo"], u["hi"]
            q = qkv[rows, lo:hi]
            k = qkv[rows, DN_WIDTH + lo:DN_WIDTH + hi]
            kb = k * u["b_c"]
            u["both"] = _dot_nt(jnp.concatenate([kb, q], axis=0).astype(BF16), k.astype(BF16))
        for u in units:
            rows, lo, hi = pl.ds(u["r0"], CHUNK), u["lo"], u["hi"]
            lmask = jnp.where(tri, jnp.exp(jnp.minimum(u["d_c"] - u["d_r"], 0.0)), 0.0)
            both = u.pop("both")
            u["neg_m"] = jnp.where(strict, -both[:CHUNK] * lmask, 0.0)
            qkm_s[rows, lo:hi] = (both[CHUNK:] * lmask).astype(BF16)

        for u in units:
            n0 = jnp.where(blk == 0, u["neg_m"], 0.0)
            u["t"] = n0 + eye
            nb = n0.astype(BF16)
            u["y"] = _dot(nb, nb).astype(BF16)
        n_steps = INV_BASE.bit_length() - 2
        for step in range(n_steps):
            for u in units:
                u["t_new"] = u["t"] + _dot(u["t"].astype(BF16), u["y"])
                if step + 1 < n_steps:
                    u["y"] = _dot(u["y"], u["y"]).astype(BF16)
            for u in units:
                u["t"] = u.pop("t_new")
        size = INV_BASE
        while size < CHUNK:
            for u in units:
                off = jnp.where(blk == size, u["neg_m"], 0.0).astype(BF16)
                u["tb"] = u["t"].astype(BF16)
                u["a"] = _dot(u["tb"], off).astype(BF16)
            for u in units:
                u["t"] = u["t"] + _dot(u.pop("a"), u.pop("tb"))
            size *= 2

        for u in units:
            rows, lo, hi = pl.ds(u["r0"], CHUNK), u["lo"], u["hi"]
            k = qkv[rows, DN_WIDTH + lo:DN_WIDTH + hi]
            v = qkv[rows, 2 * DN_WIDTH + lo:2 * DN_WIDTH + hi]
            e_d = jnp.exp(u["d_c"])
            kb = k * u["b_c"]
            rhs = jnp.concatenate([v * u["b_c"], kb * e_d], axis=1)
            sol = _dot(u["t"].astype(BF16), rhs.astype(BF16))
            u_s[rows, lo:hi] = sol[:, :LANES]
            w_s[rows, lo:hi] = sol[:, LANES:].astype(BF16)
        for u in units:
            rows, lo, hi = pl.ds(u["r0"], CHUNK), u["lo"], u["hi"]
            q = qkv[rows, lo:hi]
            k = qkv[rows, DN_WIDTH + lo:DN_WIDTH + hi]
            d_c = u["d_c"]
            d_last = d_c[CHUNK - 1:CHUNK, :]
            qd_s[rows, lo:hi] = (q * jnp.exp(d_c)).astype(BF16)
            kdt_s[u["c"], lo:hi, :] = jnp.transpose(k * jnp.exp(d_last - d_c)).astype(BF16)

    phase_a()

    def phase_b(c):
        r0 = c * CHUNK
        rows = pl.ds(r0, CHUNK)
        last = gcol[r0 + CHUNK - 1:r0 + CHUNK, :]
        heads = [dict(h=h, lo=h * LANES, hi=(h + 1) * LANES) for h in range(N_HEADS)]
        for u in heads:
            lo, hi = u["lo"], u["hi"]
            u["s_old"] = state[u["h"]]
            wq = jnp.concatenate([w_s[rows, lo:hi], qd_s[rows, lo:hi]], axis=0)
            u["ws_qs"] = _dot(wq, u["s_old"].astype(BF16))
        for u in heads:
            lo, hi = u["lo"], u["hi"]
            u["v_new"] = (u_s[rows, lo:hi] - u["ws_qs"][:CHUNK]).astype(BF16)
            cd = jnp.exp(last[:, N_HEADS + u["h"]:N_HEADS + u["h"] + 1])
            state[u["h"]] = u["s_old"] * cd + _dot(kdt_s[c, lo:hi, :], u["v_new"])
        for u in heads:
            lo, hi = u["lo"], u["hi"]
            o_s[rows, lo:hi] = u["ws_qs"][CHUNK:] + _dot(qkm_s[rows, lo:hi], u["v_new"])

    def spatial_gating(g):
        lo, hi = g * LANES, (g + 1) * LANES
        u = _gelu_tanh(guv_s[:, lo:hi])
        v = _gelu_tanh(guv_s[:, GM_WIDTH + lo:GM_WIDTH + hi])
        mu = jnp.mean(v, axis=-1, keepdims=True)
        vc = v - mu
        var = jnp.mean(vc * vc, axis=-1, keepdims=True)
        vn = (vc * lax.rsqrt(var + EPS) * lng_ref[g:g + 1, :] + lnb_ref[g:g + 1, :]).astype(BF16)
        wtri = jnp.where(tri, ws_ref[g], 0.0).astype(BF16)
        for c in range(n_chunks):
            r0, r1 = c * CHUNK, (c + 1) * CHUNK
            mixed = _dot(wtri, vn[r0:r1]) + bs_ref[:, lo:hi]
            ymix[r0:r1, lo:hi] = (u[r0:r1] * mixed).astype(BF16)

    for i in range(max(n_chunks, N_GROUPS)):
        if i < n_chunks:
            phase_b(i)
        if i < N_GROUPS:
            spatial_gating(i)

    for h in range(N_HEADS):
        lo, hi = h * LANES, (h + 1) * LANES
        ymix[:, GM_WIDTH + lo:GM_WIDTH + hi] = (_rms(o_s[:, lo:hi], onorm_ref[...])
                                                * _silu(gate_s[:, lo:hi])).astype(BF16)

    out_ref[0] = x + _dot(ymix[...], wout_ref[...])


def _mixer(x, l, params):
    b, s, d = x.shape
    ts = TS_MIX
    tile = pl.BlockSpec((1, ts, d), lambda i, j: (i, j, 0))
    consts = params
    return pl.pallas_call(
        _mixer_kernel,
        grid=(b, s // ts),
        in_specs=[tile] + [_layer_spec(a, l) for a in consts],
        out_specs=tile,
        out_shape=jax.ShapeDtypeStruct(x.shape, F32),
        scratch_shapes=[
            pltpu.VMEM((HIST + ts, 3 * DN_WIDTH), F32),
            pltpu.VMEM((N_HEADS, LANES, LANES), F32),
            pltpu.VMEM((ts, 3 * DN_WIDTH), F32),
            pltpu.VMEM((ts, LANES), F32),
            pltpu.VMEM((ts // CHUNK, LANES, CHUNK), F32),
            pltpu.VMEM((ts, DN_WIDTH), F32),
            pltpu.VMEM((ts, DN_WIDTH), BF16),
            pltpu.VMEM((ts, DN_WIDTH), BF16),
            pltpu.VMEM((ts, DN_WIDTH), BF16),
            pltpu.VMEM((ts // CHUNK, DN_WIDTH, CHUNK), BF16),
            pltpu.VMEM((ts, DN_WIDTH), F32),
            pltpu.VMEM((ts, GM_WIDTH + DN_WIDTH), BF16),
            pltpu.VMEM((ts, 2 * GM_WIDTH), F32),
            pltpu.VMEM((ts, DN_WIDTH), F32),
        ],
        compiler_params=pltpu.CompilerParams(dimension_semantics=("arbitrary", "arbitrary"),
                                             vmem_limit_bytes=VMEM_LIMIT),
        name="mixer",
    )(x, *consts)


def _xattn_kernel(x_ref, g_ref, wq_ref, k_ref, v_ref, wo_ref, out_ref, o_s):
    x = x_ref[0]
    d = x.shape[-1]
    hd = d // XA_HEADS
    hn = _rms(x, g_ref[...]).astype(BF16)
    q = _dot(hn, wq_ref[...]).astype(BF16)
    heads = [dict(lo=h * hd, hi=(h + 1) * hd) for h in range(XA_HEADS)]
    for u in heads:
        u["s"] = _dot_nt(q[:, u["lo"]:u["hi"]], k_ref[0, :, u["lo"]:u["hi"]]) * (hd ** -0.5)
    for u in heads:
        s = u.pop("s")
        e = jnp.exp(s - jnp.max(s, axis=-1, keepdims=True))
        u["p"] = (e / jnp.sum(e, axis=-1, keepdims=True)).astype(BF16)
    for u in heads:
        o_s[:, u["lo"]:u["hi"]] = _dot(u.pop("p"), v_ref[0, :, u["lo"]:u["hi"]]).astype(BF16)
    out_ref[0] = x + _dot(o_s[...], wo_ref[...])


def _xattn(x, l, g, wq, k, v, wo):
    b, s, d = x.shape
    nm = k.shape[1]
    ts = TS_XA
    tile = pl.BlockSpec((1, ts, d), lambda i, j: (i, j, 0))
    kv = pl.BlockSpec((1, nm, d), lambda i, j: (i, 0, 0))
    return pl.pallas_call(
        _xattn_kernel,
        grid=(b, s // ts),
        in_specs=[tile, _layer_spec(g, l), _layer_spec(wq, l), kv, kv, _layer_spec(wo, l)],
        out_specs=tile,
        out_shape=jax.ShapeDtypeStruct(x.shape, F32),
        scratch_shapes=[pltpu.VMEM((ts, d), BF16)],
        compiler_params=pltpu.CompilerParams(dimension_semantics=("arbitrary", "arbitrary"),
                                             vmem_limit_bytes=VMEM_LIMIT),
        name="xattn",
    )(x, g, wq, k, v, wo)


def _ffn_kernel(x_ref, g_ref, w1_ref, w2_ref, gf_ref, out_ref, *, final_norm):
    x = x_ref[...]
    hn = _rms(x, g_ref[...]).astype(BF16)
    acc = x
    for j in range(w1_ref.shape[1] // FF_BLOCK):
        lo, hi = j * FF_BLOCK, (j + 1) * FF_BLOCK
        a = jnp.maximum(_dot(hn, w1_ref[:, lo:hi]), 0.0)
        acc = acc + _dot((a * a).astype(BF16), w2_ref[lo:hi, :])
    if final_norm:
        acc = _rms(acc, gf_ref[...])
    out_ref[...] = acc


def _ffn(x2, l, g, w1, w2, gf, final_norm):
    t, d = x2.shape
    tm = TM_FFN
    tile = pl.BlockSpec((tm, d), lambda i: (i, 0))
    return pl.pallas_call(
        functools.partial(_ffn_kernel, final_norm=final_norm),
        grid=(t // tm,),
        in_specs=[tile, _layer_spec(g, l), _layer_spec(w1, l), _layer_spec(w2, l), _layer_spec(gf, 0)],
        out_specs=tile,
        out_shape=jax.ShapeDtypeStruct(x2.shape, F32),
        compiler_params=pltpu.CompilerParams(dimension_semantics=("arbitrary",), vmem_limit_bytes=VMEM_LIMIT),
        name="ffn_final" if final_norm else "ffn",
    )(x2, g, w1, w2, gf)


def kernel(x, mem, norm_mix, w_in, gm_ln_g, gm_ln_b, gm_ws, gm_bs, dn_conv, dn_a_log, dn_dt_bias, dn_onorm,
           w_out, norm_xa, norm_mem, xa_wq, xa_wk, xa_wv, xa_wo, norm_ffn, ffn_w1, ffn_w2, norm_final):
    b, s, d = x.shape
    depth = w_in.shape[0]
    assert s % TS_MIX == 0 and s % TS_XA == 0 and (b * s) % TM_FFN == 0 and TS_MIX % CHUNK == 0
    assert w_in.shape[2] == C_AB + 2 * N_HEADS
    rows = lambda a: a.reshape(a.shape[0], 1, -1)
    w_in_p = jnp.pad(w_in, ((0, 0), (0, 0), (0, D_IN_PADDED - w_in.shape[2]))).astype(BF16)
    bs_full = jnp.repeat(jnp.swapaxes(gm_bs, 1, 2), LANES, axis=2)
    pad_heads = lambda a: jnp.pad(rows(a), ((0, 0), (0, 0), (N_HEADS, LANES - 2 * N_HEADS)))
    mixer_params = (rows(norm_mix), w_in_p, gm_ln_g, gm_ln_b, gm_ws, bs_full, dn_conv, pad_heads(dn_a_log),
                    pad_heads(dn_dt_bias), rows(dn_onorm), w_out.astype(BF16))
    wq, wk, wv, wo = (w.astype(BF16) for w in (xa_wq, xa_wk, xa_wv, xa_wo))
    w1, w2 = ffn_w1.astype(BF16), ffn_w2.astype(BF16)
    g_xa, g_mem, g_ffn, g_final = rows(norm_xa), rows(norm_mem), rows(norm_ffn), norm_final.reshape(1, 1, -1)
    for l in range(depth):
        x = _mixer(x, l, mixer_params)
        k, v = _memkv(mem, l, g_mem, wk, wv)
        x = _xattn(x, l, g_xa, wq, k, v, wo)
        x = _ffn(x.reshape(b * s, d), l, g_ffn, w1, w2, g_final, final_norm=(l == depth - 1)).reshape(b, s, d)
    return x
```

```python
import functools
import math

import jax
import jax.numpy as jnp
from jax import lax
from jax.experimental import pallas as pl
from jax.experimental.pallas import tpu as pltpu

F32 = jnp.float32
BF16 = jnp.bfloat16

EPS = 1e-6
LANES = 128
CHUNK = 128
INV_BASE = 8
N_GROUPS = 4
N_HEADS = 4
GM_WIDTH = N_GROUPS * LANES
DN_WIDTH = N_HEADS * LANES
CONV_W = 4
HIST = 8
XA_HEADS = 4
VMEM_LIMIT = 56 * 1024 * 1024

TS_MIX = 512
TS_XA = 1024
TM_FFN = 1024
FF_BLOCK = 1024
PROJ_BLOCK = 256

C_QKV = 2 * GM_WIDTH
C_GATE = C_QKV + 3 * DN_WIDTH
C_AB = C_GATE + DN_WIDTH
D_IN_PADDED = C_AB + LANES


def _layer_spec(a, l):
    zeros = (0,) * (a.ndim - 1)
    return pl.BlockSpec((None,) + a.shape[1:], lambda *_: (l,) + zeros, pipeline_mode=pl.Buffered(1))


def _rms(x, g):
    return x * lax.rsqrt(jnp.mean(x * x, axis=-1, keepdims=True) + EPS) * g


def _gelu_tanh(x):
    c = math.sqrt(2.0 / math.pi)
    return x * (0.5 * (1.0 + jnp.tanh(c * (x + 0.044715 * (x * x * x)))))


def _sigmoid(x):
    return 1.0 / (1.0 + jnp.exp(-x))


def _silu(x):
    return x * _sigmoid(x)


def _softplus(x):
    return jnp.maximum(x, 0.0) + jnp.log1p(jnp.exp(-jnp.abs(x)))


def _dot(a, b):
    return jnp.dot(a, b, preferred_element_type=F32)


def _dot_nt(a, b):
    return lax.dot_general(a, b, (((1,), (1,)), ((), ())), preferred_element_type=F32)


def _memkv_kernel(mem_ref, g_ref, wk_ref, wv_ref, k_ref, v_ref):
    m = _rms(mem_ref[0], g_ref[...]).astype(BF16)
    k_ref[0] = _dot(m, wk_ref[...]).astype(BF16)
    v_ref[0] = _dot(m, wv_ref[...]).astype(BF16)


def _memkv(mem, l, g, wk, wv):
    b, nm, d = mem.shape
    return pl.pallas_call(
        _memkv_kernel,
        grid=(b,),
        in_specs=[pl.BlockSpec((1, nm, d), lambda i: (i, 0, 0))] + [_layer_spec(a, l) for a in (g, wk, wv)],
        out_specs=[pl.BlockSpec((1, nm, d), lambda i: (i, 0, 0))] * 2,
        out_shape=[jax.ShapeDtypeStruct((b, nm, d), BF16)] * 2,
        compiler_params=pltpu.CompilerParams(dimension_semantics=("arbitrary",), vmem_limit_bytes=VMEM_LIMIT),
        name="memkv",
    )(mem, g, wk, wv)


def _mixer_kernel(x_ref, nrm_ref, win_ref, lng_ref, lnb_ref, ws_ref, bs_ref,
                  conv_ref, alog_ref, dtb_ref, onorm_ref, wout_ref,
                  out_ref,
                  cbuf, state, qkv, gcol, drow, u_s, w_s, qkm_s, qd_s, kdt_s, o_s, ymix, guv_s, gate_s):
    ts = x_ref.shape[1]
    n_chunks = ts // CHUNK
    wgm_ref = win_ref.at[:, 0:C_QKV]
    wqkv_ref = win_ref.at[:, C_QKV:C_GATE]
    wgate_ref = win_ref.at[:, C_GATE:C_AB]
    wab_ref = win_ref.at[:, C_AB:D_IN_PADDED]

    @pl.when(pl.program_id(1) == 0)
    def _():
        cbuf[0:HIST, :] = jnp.zeros((HIST, 3 * DN_WIDTH), F32)
        state[...] = jnp.zeros_like(state)

    x = x_ref[0]
    hn = _rms(x, nrm_ref[...]).astype(BF16)

    row = lax.broadcasted_iota(jnp.int32, (CHUNK, CHUNK), 0)
    col = lax.broadcasted_iota(jnp.int32, (CHUNK, CHUNK), 1)
    tri = row >= col
    strict = row > col
    eye = jnp.where(row == col, 1.0, 0.0)
    rc = row ^ col
    blk = jnp.zeros((CHUNK, CHUNK), jnp.int32)
    size = INV_BASE
    while size < CHUNK:
        blk = jnp.where(rc >= size, size, blk)
        size *= 2

    cbuf[HIST:HIST + ts, :] = _dot(hn, wqkv_ref[...])
    ab = _dot(hn, wab_ref[...])
    beta = _sigmoid(ab)
    gdec = -jnp.exp(alog_ref[...]) * _softplus(ab + dtb_ref[...])
    pos = lax.broadcasted_iota(jnp.int32, (ts, LANES), 0) % CHUNK
    d = gdec
    shift = 1
    while shift < CHUNK:
        d = d + jnp.where(pos >= shift, pltpu.roll(d, shift, axis=0), 0.0)
        shift *= 2
    lane = lax.broadcasted_iota(jnp.int32, (ts, LANES), 1)
    gcol[...] = jnp.where(lane < N_HEADS, beta, d)
    for c in range(n_chunks):
        drow[c] = jnp.transpose(d[c * CHUNK:(c + 1) * CHUNK])

    def conv_block(lo, hi):
        xe = cbuf[:, lo:hi]
        x1 = pltpu.roll(xe, 1, axis=0)
        z = xe * conv_ref[1:2, lo:hi] + x1 * conv_ref[0:1, lo:hi]
        y = xe * conv_ref[3:4, lo:hi] + x1 * conv_ref[2:3, lo:hi] + pltpu.roll(z, 2, axis=0)
        return _silu(y[HIST:])

    def unit_norm(a):
        return a * lax.rsqrt(jnp.sum(a * a, axis=-1, keepdims=True) + EPS)

    def proj_block(dst, w_ref, t):
        dst[:, t * PROJ_BLOCK:(t + 1) * PROJ_BLOCK] = _dot(hn, w_ref[:, t * PROJ_BLOCK:(t + 1) * PROJ_BLOCK])

    mxu_pieces = [functools.partial(proj_block, guv_s, wgm_ref, t) for t in range(2 * GM_WIDTH // PROJ_BLOCK)]
    mxu_pieces += [functools.partial(proj_block, gate_s, wgate_ref, t) for t in range(DN_WIDTH // PROJ_BLOCK)]
    n_blocks = 3 * DN_WIDTH // PROJ_BLOCK
    for p in range(n_blocks):
        lo = p * PROJ_BLOCK
        act = conv_block(lo, lo + PROJ_BLOCK)
        for s0 in range(0, PROJ_BLOCK, LANES):
            a = act[:, s0:s0 + LANES]
            if lo < DN_WIDTH:
                a = unit_norm(a) * (LANES ** -0.5)
            elif lo < 2 * DN_WIDTH:
                a = unit_norm(a)
            qkv[:, lo + s0:lo + s0 + LANES] = a
        if p < len(mxu_pieces):
            mxu_pieces[p]()
    for piece in mxu_pieces[n_blocks:]:
        piece()
    cbuf[0:HIST, :] = cbuf[ts:ts + HIST, :]

    def phase_a():
        units = []
        for c in range(n_chunks):
            r0 = c * CHUNK
            gc = gcol[pl.ds(r0, CHUNK), :]
            dr = drow[c]
            for h in range(N_HEADS):
                units.append(dict(c=c, r0=r0, lo=h * LANES, hi=(h + 1) * LANES,
                                  b_c=gc[:, h:h + 1],
                                  d_c=gc[:, N_HEADS + h:N_HEADS + h + 1],
                                  d_r=dr[N_HEADS + h:N_HEADS + h + 1, :]))

        for u in units:
            rows, lo, hi = pl.ds(u["r0"], CHUNK), u["lo"], u["hi"]
            q = qkv[rows, lo:hi]
            k = qkv[rows, DN_WIDTH + lo:DN_WIDTH + hi]
            kb = k * u["b_c"]
            u["both"] = _dot_nt(jnp.concatenate([kb, q], axis=0).astype(BF16), k.astype(BF16))
        for u in units:
            rows, lo, hi = pl.ds(u["r0"], CHUNK), u["lo"], u["hi"]
            lmask = jnp.where(tri, jnp.exp(jnp.minimum(u["d_c"] - u["d_r"], 0.0)), 0.0)
            both = u.pop("both")
            u["neg_m"] = jnp.where(strict, -both[:CHUNK] * lmask, 0.0)
            qkm_s[rows, lo:hi] = (both[CHUNK:] * lmask).astype(BF16)

        for u in units:
            n0 = jnp.where(blk == 0, u["neg_m"], 0.0)
            u["t"] = n0 + eye
            nb = n0.astype(BF16)
            u["y"] = _dot(nb, nb).astype(BF16)
        n_steps = INV_BASE.bit_length() - 2
        for step in range(n_steps):
            for u in units:
                u["t_new"] = u["t"] + _dot(u["t"].astype(BF16), u["y"])
                if step + 1 < n_steps:
                    u["y"] = _dot(u["y"], u["y"]).astype(BF16)
            for u in units:
                u["t"] = u.pop("t_new")
        size = INV_BASE
        while size < CHUNK:
            for u in units:
                off = jnp.where(blk == size, u["neg_m"], 0.0).astype(BF16)
                u["tb"] = u["t"].astype(BF16)
                u["a"] = _dot(u["tb"], off).astype(BF16)
            for u in units:
                u["t"] = u["t"] + _dot(u.pop("a"), u.pop("tb"))
            size *= 2

        for u in units:
            rows, lo, hi = pl.ds(u["r0"], CHUNK), u["lo"], u["hi"]
            k = qkv[rows, DN_WIDTH + lo:DN_WIDTH + hi]
            v = qkv[rows, 2 * DN_WIDTH + lo:2 * DN_WIDTH + hi]
            e_d = jnp.exp(u["d_c"])
            kb = k * u["b_c"]
            rhs = jnp.concatenate([v * u["b_c"], kb * e_d], axis=1)
            sol = _dot(u["t"].astype(BF16), rhs.astype(BF16))
            u_s[rows, lo:hi] = sol[:, :LANES]
            w_s[rows, lo:hi] = sol[:, LANES:].astype(BF16)
        for u in units:
            rows, lo, hi = pl.ds(u["r0"], CHUNK), u["lo"], u["hi"]
            q = qkv[rows, lo:hi]
            k = qkv[rows, DN_WIDTH + lo:DN_WIDTH + hi]
            d_c = u["d_c"]
            d_last = d_c[CHUNK - 1:CHUNK, :]
            qd_s[rows, lo:hi] = (q * jnp.exp(d_c)).astype(BF16)
            kdt_s[u["c"], lo:hi, :] = jnp.transpose(k * jnp.exp(d_last - d_c)).astype(BF16)

    phase_a()

    def phase_b(c):
        r0 = c * CHUNK
        rows = pl.ds(r0, CHUNK)
        last = gcol[r0 + CHUNK - 1:r0 + CHUNK, :]
        heads = [dict(h=h, lo=h * LANES, hi=(h + 1) * LANES) for h in range(N_HEADS)]
        for u in heads:
            lo, hi = u["lo"], u["hi"]
            u["s_old"] = state[u["h"]]
            wq = jnp.concatenate([w_s[rows, lo:hi], qd_s[rows, lo:hi]], axis=0)
            u["ws_qs"] = _dot(wq, u["s_old"].astype(BF16))
        for u in heads:
            lo, hi = u["lo"], u["hi"]
            u["v_new"] = (u_s[rows, lo:hi] - u["ws_qs"][:CHUNK]).astype(BF16)
            cd = jnp.exp(last[:, N_HEADS + u["h"]:N_HEADS + u["h"] + 1])
            state[u["h"]] = u["s_old"] * cd + _dot(kdt_s[c, lo:hi, :], u["v_new"])
        for u in heads:
            lo, hi = u["lo"], u["hi"]
            o_s[rows, lo:hi] = u["ws_qs"][CHUNK:] + _dot(qkm_s[rows, lo:hi], u["v_new"])

    def spatial_gating(g):
        lo, hi = g * LANES, (g + 1) * LANES
        u = _gelu_tanh(guv_s[:, lo:hi])
        v = _gelu_tanh(guv_s[:, GM_WIDTH + lo:GM_WIDTH + hi])
        mu = jnp.mean(v, axis=-1, keepdims=True)
        vc = v - mu
        var = jnp.mean(vc * vc, axis=-1, keepdims=True)
        vn = (vc * lax.rsqrt(var + EPS) * lng_ref[g:g + 1, :] + lnb_ref[g:g + 1, :]).astype(BF16)
        wtri = jnp.where(tri, ws_ref[g], 0.0).astype(BF16)
        for c in range(n_chunks):
            r0, r1 = c * CHUNK, (c + 1) * CHUNK
            mixed = _dot(wtri, vn[r0:r1]) + bs_ref[:, lo:hi]
            ymix[r0:r1, lo:hi] = (u[r0:r1] * mixed).astype(BF16)

    for i in range(max(n_chunks, N_GROUPS)):
        if i < n_chunks:
            phase_b(i)
        if i < N_GROUPS:
            spatial_gating(i)

    for h in range(N_HEADS):
        lo, hi = h * LANES, (h + 1) * LANES
        ymix[:, GM_WIDTH + lo:GM_WIDTH + hi] = (_rms(o_s[:, lo:hi], onorm_ref[...])
                                                * _silu(gate_s[:, lo:hi])).astype(BF16)

    out_ref[0] = x + _dot(ymix[...], wout_ref[...])


def _mixer(x, l, params):
    b, s, d = x.shape
    ts = TS_MIX
    tile = pl.BlockSpec((1, ts, d), lambda i, j: (i, j, 0))
    consts = params
    return pl.pallas_call(
        _mixer_kernel,
        grid=(b, s // ts),
        in_specs=[tile] + [_layer_spec(a, l) for a in consts],
        out_specs=tile,
        out_shape=jax.ShapeDtypeStruct(x.shape, F32),
        scratch_shapes=[
            pltpu.VMEM((HIST + ts, 3 * DN_WIDTH), F32),
            pltpu.VMEM((N_HEADS, LANES, LANES), F32),
            pltpu.VMEM((ts, 3 * DN_WIDTH), F32),
            pltpu.VMEM((ts, LANES), F32),
            pltpu.VMEM((ts // CHUNK, LANES, CHUNK), F32),
            pltpu.VMEM((ts, DN_WIDTH), F32),
            pltpu.VMEM((ts, DN_WIDTH), BF16),
            pltpu.VMEM((ts, DN_WIDTH), BF16),
            pltpu.VMEM((ts, DN_WIDTH), BF16),
            pltpu.VMEM((ts // CHUNK, DN_WIDTH, CHUNK), BF16),
            pltpu.VMEM((ts, DN_WIDTH), F32),
            pltpu.VMEM((ts, GM_WIDTH + DN_WIDTH), BF16),
            pltpu.VMEM((ts, 2 * GM_WIDTH), F32),
            pltpu.VMEM((ts, DN_WIDTH), F32),
        ],
        compiler_params=pltpu.CompilerParams(dimension_semantics=("arbitrary", "arbitrary"),
                                             vmem_limit_bytes=VMEM_LIMIT),
        name="mixer",
    )(x, *consts)


def _xattn_kernel(x_ref, g_ref, wq_ref, k_ref, v_ref, wo_ref, out_ref, o_s):
    x = x_ref[0]
    d = x.shape[-1]
    hd = d // XA_HEADS
    hn = _rms(x, g_ref[...]).astype(BF16)
    q = _dot(hn, wq_ref[...]).astype(BF16)
    heads = [dict(lo=h * hd, hi=(h + 1) * hd) for h in range(XA_HEADS)]
    for u in heads:
        u["s"] = _dot_nt(q[:, u["lo"]:u["hi"]], k_ref[0, :, u["lo"]:u["hi"]]) * (hd ** -0.5)
    for u in heads:
        s = u.pop("s")
        e = jnp.exp(s - jnp.max(s, axis=-1, keepdims=True))
        u["p"] = (e / jnp.sum(e, axis=-1, keepdims=True)).astype(BF16)
    for u in heads:
        o_s[:, u["lo"]:u["hi"]] = _dot(u.pop("p"), v_ref[0, :, u["lo"]:u["hi"]]).astype(BF16)
    out_ref[0] = x + _dot(o_s[...], wo_ref[...])


def _xattn(x, l, g, wq, k, v, wo):
    b, s, d = x.shape
    nm = k.shape[1]
    ts = TS_XA
    tile = pl.BlockSpec((1, ts, d), lambda i, j: (i, j, 0))
    kv = pl.BlockSpec((1, nm, d), lambda i, j: (i, 0, 0))
    return pl.pallas_call(
        _xattn_kernel,
        grid=(b, s // ts),
        in_specs=[tile, _layer_spec(g, l), _layer_spec(wq, l), kv, kv, _layer_spec(wo, l)],
        out_specs=tile,
        out_shape=jax.ShapeDtypeStruct(x.shape, F32),
        scratch_shapes=[pltpu.VMEM((ts, d), BF16)],
        compiler_params=pltpu.CompilerParams(dimension_semantics=("arbitrary", "arbitrary"),
                                             vmem_limit_bytes=VMEM_LIMIT),
        name="xattn",
    )(x, g, wq, k, v, wo)


def _ffn_kernel(x_ref, g_ref, w1_ref, w2_ref, gf_ref, out_ref, *, final_norm):
    x = x_ref[...]
    hn = _rms(x, g_ref[...]).astype(BF16)
    acc = x
    for j in range(w1_ref.shape[1] // FF_BLOCK):
        lo, hi = j * FF_BLOCK, (j + 1) * FF_BLOCK
        a = jnp.maximum(_dot(hn, w1_ref[:, lo:hi]), 0.0)
        acc = acc + _dot((a * a).astype(BF16), w2_ref[lo:hi, :])
    if final_norm:
        acc = _rms(acc, gf_ref[...])
    out_ref[...] = acc


def _ffn(x2, l, g, w1, w2, gf, final_norm):
    t, d = x2.shape
    tm = TM_FFN
    tile = pl.BlockSpec((tm, d), lambda i: (i, 0))
    return pl.pallas_call(
        functools.partial(_ffn_kernel, final_norm=final_norm),
        grid=(t // tm,),
        in_specs=[tile, _layer_spec(g, l), _layer_spec(w1, l), _layer_spec(w2, l), _layer_spec(gf, 0)],
        out_specs=tile,
        out_shape=jax.ShapeDtypeStruct(x2.shape, F32),
        compiler_params=pltpu.CompilerParams(dimension_semantics=("arbitrary",), vmem_limit_bytes=VMEM_LIMIT),
        name="ffn_final" if final_norm else "ffn",
    )(x2, g, w1, w2, gf)


def kernel(x, mem, norm_mix, w_in, gm_ln_g, gm_ln_b, gm_ws, gm_bs, dn_conv, dn_a_log, dn_dt_bias, dn_onorm,
           w_out, norm_xa, norm_mem, xa_wq, xa_wk, xa_wv, xa_wo, norm_ffn, ffn_w1, ffn_w2, norm_final):
    b, s, d = x.shape
    depth = w_in.shape[0]
    assert s % TS_MIX == 0 and s % TS_XA == 0 and (b * s) % TM_FFN == 0 and TS_MIX % CHUNK == 0
    assert w_in.shape[2] == C_AB + 2 * N_HEADS
    rows = lambda a: a.reshape(a.shape[0], 1, -1)
    w_in_p = jnp.pad(w_in, ((0, 0), (0, 0), (0, D_IN_PADDED - w_in.shape[2]))).astype(BF16)
    bs_full = jnp.repeat(jnp.swapaxes(gm_bs, 1, 2), LANES, axis=2)
    pad_heads = lambda a: jnp.pad(rows(a), ((0, 0), (0, 0), (N_HEADS, LANES - 2 * N_HEADS)))
    mixer_params = (rows(norm_mix), w_in_p, gm_ln_g, gm_ln_b, gm_ws, bs_full, dn_conv, pad_heads(dn_a_log),
                    pad_heads(dn_dt_bias), rows(dn_onorm), w_out.astype(BF16))
    wq, wk, wv, wo = (w.astype(BF16) for w in (xa_wq, xa_wk, xa_wv, xa_wo))
    w1, w2 = ffn_w1.astype(BF16), ffn_w2.astype(BF16)
    g_xa, g_mem, g_ffn, g_final = rows(norm_xa), rows(norm_mem), rows(norm_ffn), norm_final.reshape(1, 1, -1)
    for l in range(depth):
        x = _mixer(x, l, mixer_params)
        k, v = _memkv(mem, l, g_mem, wk, wv)
        x = _xattn(x, l, g_xa, wq, k, v, wo)
        x = _ffn(x.reshape(b * s, d), l, g_ffn, w1, w2, g_final, final_norm=(l == depth - 1)).reshape(b, s, d)
    return x
```

```python
import functools
import math

import jax
import jax.numpy as jnp
from jax import lax
from jax.experimental import pallas as pl
from jax.experimental.pallas import tpu as pltpu

F32 = jnp.float32
BF16 = jnp.bfloat16

EPS = 1e-6
LANES = 128
CHUNK = 128
INV_BASE = 8
N_GROUPS = 4
N_HEADS = 4
GM_WIDTH = N_GROUPS * LANES
DN_WIDTH = N_HEADS * LANES
CONV_W = 4
HIST = 8
XA_HEADS = 4
VMEM_LIMIT = 56 * 1024 * 1024

TS_MIX = 512
TS_XA = 1024
TM_FFN = 1024
FF_BLOCK = 1024
PROJ_BLOCK = 256

C_QKV = 2 * GM_WIDTH
C_GATE = C_QKV + 3 * DN_WIDTH
C_AB = C_GATE + DN_WIDTH
D_IN_PADDED = C_AB + LANES


def _layer_spec(a, l):
    zeros = (0,) * (a.ndim - 1)
    return pl.BlockSpec((None,) + a.shape[1:], lambda *_: (l,) + zeros, pipeline_mode=pl.Buffered(1))


def _rms(x, g):
    return x * lax.rsqrt(jnp.mean(x * x, axis=-1, keepdims=True) + EPS) * g


def _gelu_tanh(x):
    c = math.sqrt(2.0 / math.pi)
    return x * (0.5 * (1.0 + jnp.tanh(c * (x + 0.044715 * (x * x * x)))))


def _sigmoid(x):
    return 1.0 / (1.0 + jnp.exp(-x))


def _silu(x):
    return x * _sigmoid(x)


def _softplus(x):
    return jnp.maximum(x, 0.0) + jnp.log1p(jnp.exp(-jnp.abs(x)))


def _dot(a, b):
    return jnp.dot(a, b, preferred_element_type=F32)


def _dot_nt(a, b):
    return lax.dot_general(a, b, (((1,), (1,)), ((), ())), preferred_element_type=F32)


def _memkv_kernel(mem_ref, g_ref, wk_ref, wv_ref, k_ref, v_ref):
    m = _rms(mem_ref[0], g_ref[...]).astype(BF16)
    k_ref[0] = _dot(m, wk_ref[...]).astype(BF16)
    v_ref[0] = _dot(m, wv_ref[...]).astype(BF16)


def _memkv(mem, g, wk, wv):
    b, nm, d = mem.shape
    depth = wk.shape[0]
    per_layer = lambda a: pl.BlockSpec((None,) + a.shape[1:], lambda l, i: (l,) + (0,) * (a.ndim - 1))
    out = pl.BlockSpec((None, 1, nm, d), lambda l, i: (l, i, 0, 0))
    return pl.pallas_call(
        _memkv_kernel,
        grid=(depth, b),
        in_specs=[pl.BlockSpec((1, nm, d), lambda l, i: (i, 0, 0))] + [per_layer(a) for a in (g, wk, wv)],
        out_specs=[out] * 2,
        out_shape=[jax.ShapeDtypeStruct((depth, b, nm, d), BF16)] * 2,
        compiler_params=pltpu.CompilerParams(dimension_semantics=("arbitrary", "arbitrary"),
                                             vmem_limit_bytes=VMEM_LIMIT),
        name="memkv",
    )(mem, g, wk, wv)


def _mixer_kernel(x_ref, nrm_ref, win_ref, lng_ref, lnb_ref, ws_ref, bs_ref,
                  conv_ref, alog_ref, dtb_ref, onorm_ref, wout_ref,
                  out_ref,
                  cbuf, state, qkv, gcol, drow, u_s, w_s, qkm_s, qd_s, kdt_s, o_s, ymix, guv_s, gate_s):
    ts = x_ref.shape[1]
    n_chunks = ts // CHUNK
    wgm_ref = win_ref.at[:, 0:C_QKV]
    wqkv_ref = win_ref.at[:, C_QKV:C_GATE]
    wgate_ref = win_ref.at[:, C_GATE:C_AB]
    wab_ref = win_ref.at[:, C_AB:D_IN_PADDED]

    @pl.when(pl.program_id(1) == 0)
    def _():
        cbuf[0:HIST, :] = jnp.zeros((HIST, 3 * DN_WIDTH), F32)
        state[...] = jnp.zeros_like(state)

    x = x_ref[0]
    hn = _rms(x, nrm_ref[...]).astype(BF16)

    row = lax.broadcasted_iota(jnp.int32, (CHUNK, CHUNK), 0)
    col = lax.broadcasted_iota(jnp.int32, (CHUNK, CHUNK), 1)
    tri = row >= col
    strict = row > col
    eye = jnp.where(row == col, 1.0, 0.0)
    rc = row ^ col
    blk = jnp.zeros((CHUNK, CHUNK), jnp.int32)
    size = INV_BASE
    while size < CHUNK:
        blk = jnp.where(rc >= size, size, blk)
        size *= 2

    cbuf[HIST:HIST + ts, :] = _dot(hn, wqkv_ref[...])
    ab = _dot(hn, wab_ref[...])
    beta = _sigmoid(ab)
    gdec = -jnp.exp(alog_ref[...]) * _softplus(ab + dtb_ref[...])
    pos = lax.broadcasted_iota(jnp.int32, (ts, LANES), 0) % CHUNK
    d = gdec
    shift = 1
    while shift < CHUNK:
        d = d + jnp.where(pos >= shift, pltpu.roll(d, shift, axis=0), 0.0)
        shift *= 2
    lane = lax.broadcasted_iota(jnp.int32, (ts, LANES), 1)
    gcol[...] = jnp.where(lane < N_HEADS, beta, d)
    for c in range(n_chunks):
        drow[c] = jnp.transpose(d[c * CHUNK:(c + 1) * CHUNK])

    def conv_block(lo, hi):
        xe = cbuf[:, lo:hi]
        x1 = pltpu.roll(xe, 1, axis=0)
        z = xe * conv_ref[1:2, lo:hi] + x1 * conv_ref[0:1, lo:hi]
        y = xe * conv_ref[3:4, lo:hi] + x1 * conv_ref[2:3, lo:hi] + pltpu.roll(z, 2, axis=0)
        return _silu(y[HIST:])

    def unit_norm(a):
        return a * lax.rsqrt(jnp.sum(a * a, axis=-1, keepdims=True) + EPS)

    def proj_block(dst, w_ref, t):
        dst[:, t * PROJ_BLOCK:(t + 1) * PROJ_BLOCK] = _dot(hn, w_ref[:, t * PROJ_BLOCK:(t + 1) * PROJ_BLOCK])

    mxu_pieces = [functools.partial(proj_block, guv_s, wgm_ref, t) for t in range(2 * GM_WIDTH // PROJ_BLOCK)]
    mxu_pieces += [functools.partial(proj_block, gate_s, wgate_ref, t) for t in range(DN_WIDTH // PROJ_BLOCK)]
    n_blocks = 3 * DN_WIDTH // PROJ_BLOCK
    for p in range(n_blocks):
        lo = p * PROJ_BLOCK
        act = conv_block(lo, lo + PROJ_BLOCK)
        for s0 in range(0, PROJ_BLOCK, LANES):
            a = act[:, s0:s0 + LANES]
            if lo < DN_WIDTH:
                a = unit_norm(a) * (LANES ** -0.5)
            elif lo < 2 * DN_WIDTH:
                a = unit_norm(a)
            qkv[:, lo + s0:lo + s0 + LANES] = a
        if p < len(mxu_pieces):
            mxu_pieces[p]()
    for piece in mxu_pieces[n_blocks:]:
        piece()
    cbuf[0:HIST, :] = cbuf[ts:ts + HIST, :]

    def phase_a():
        units = []
        for c in range(n_chunks):
            r0 = c * CHUNK
            gc = gcol[pl.ds(r0, CHUNK), :]
            dr = drow[c]
            for h in range(N_HEADS):
                units.append(dict(c=c, r0=r0, lo=h * LANES, hi=(h + 1) * LANES,
                                  b_c=gc[:, h:h + 1],
                                  d_c=gc[:, N_HEADS + h:N_HEADS + h + 1],
                                  d_r=dr[N_HEADS + h:N_HEADS + h + 1, :]))

        for u in units:
            rows, lo, hi = pl.ds(u["r0"], CHUNK), u["lo"], u["hi"]
            q = qkv[rows, lo:hi]
            k = qkv[rows, DN_WIDTH + lo:DN_WIDTH + hi]
            kb = k * u["b_c"]
            u["both"] = _dot_nt(jnp.concatenate([kb, q], axis=0).astype(BF16), k.astype(BF16))
        for u in units:
            rows, lo, hi = pl.ds(u["r0"], CHUNK), u["lo"], u["hi"]
            lmask = jnp.where(tri, jnp.exp(jnp.minimum(u["d_c"] - u["d_r"], 0.0)), 0.0)
            both = u.pop("both")
            u["neg_m"] = jnp.where(strict, -both[:CHUNK] * lmask, 0.0)
            qkm_s[rows, lo:hi] = (both[CHUNK:] * lmask).astype(BF16)

        for u in units:
            n0 = jnp.where(blk == 0, u["neg_m"], 0.0)
            u["t"] = n0 + eye
            nb = n0.astype(BF16)
            u["y"] = _dot(nb, nb).astype(BF16)
        n_steps = INV_BASE.bit_length() - 2
        for step in range(n_steps):
            for u in units:
                u["t_new"] = u["t"] + _dot(u["t"].astype(BF16), u["y"])
                if step + 1 < n_steps:
                    u["y"] = _dot(u["y"], u["y"]).astype(BF16)
            for u in units:
                u["t"] = u.pop("t_new")
        size = INV_BASE
        while size < CHUNK:
            for u in units:
                off = jnp.where(blk == size, u["neg_m"], 0.0).astype(BF16)
                u["tb"] = u["t"].astype(BF16)
                u["a"] = _dot(u["tb"], off).astype(BF16)
            for u in units:
                u["t"] = u["t"] + _dot(u.pop("a"), u.pop("tb"))
            size *= 2

        for u in units:
            rows, lo, hi = pl.ds(u["r0"], CHUNK), u["lo"], u["hi"]
            k = qkv[rows, DN_WIDTH + lo:DN_WIDTH + hi]
            v = qkv[rows, 2 * DN_WIDTH + lo:2 * DN_WIDTH + hi]
            e_d = jnp.exp(u["d_c"])
            kb = k * u["b_c"]
            rhs = jnp.concatenate([v * u["b_c"], kb * e_d], axis=1)
            sol = _dot(u["t"].astype(BF16), rhs.astype(BF16))
            u_s[rows, lo:hi] = sol[:, :LANES]
            w_s[rows, lo:hi] = sol[:, LANES:].astype(BF16)
        for u in units:
            rows, lo, hi = pl.ds(u["r0"], CHUNK), u["lo"], u["hi"]
            q = qkv[rows, lo:hi]
            k = qkv[rows, DN_WIDTH + lo:DN_WIDTH + hi]
            d_c = u["d_c"]
            d_last = d_c[CHUNK - 1:CHUNK, :]
            qd_s[rows, lo:hi] = (q * jnp.exp(d_c)).astype(BF16)
            kdt_s[u["c"], lo:hi, :] = jnp.transpose(k * jnp.exp(d_last - d_c)).astype(BF16)

    phase_a()

    def phase_b(c):
        r0 = c * CHUNK
        rows = pl.ds(r0, CHUNK)
        last = gcol[r0 + CHUNK - 1:r0 + CHUNK, :]
        heads = [dict(h=h, lo=h * LANES, hi=(h + 1) * LANES) for h in range(N_HEADS)]
        for u in heads:
            lo, hi = u["lo"], u["hi"]
            u["s_old"] = state[u["h"]]
            wq = jnp.concatenate([w_s[rows, lo:hi], qd_s[rows, lo:hi]], axis=0)
            u["ws_qs"] = _dot(wq, u["s_old"].astype(BF16))
        for u in heads:
            lo, hi = u["lo"], u["hi"]
            u["v_new"] = (u_s[rows, lo:hi] - u["ws_qs"][:CHUNK]).astype(BF16)
            cd = jnp.exp(last[:, N_HEADS + u["h"]:N_HEADS + u["h"] + 1])
            state[u["h"]] = u["s_old"] * cd + _dot(kdt_s[c, lo:hi, :], u["v_new"])
        for u in heads:
            lo, hi = u["lo"], u["hi"]
            o_s[rows, lo:hi] = u["ws_qs"][CHUNK:] + _dot(qkm_s[rows, lo:hi], u["v_new"])

    def spatial_gating(g):
        lo, hi = g * LANES, (g + 1) * LANES
        u = _gelu_tanh(guv_s[:, lo:hi])
        v = _gelu_tanh(guv_s[:, GM_WIDTH + lo:GM_WIDTH + hi])
        mu = jnp.mean(v, axis=-1, keepdims=True)
        vc = v - mu
        var = jnp.mean(vc * vc, axis=-1, keepdims=True)
        vn = (vc * lax.rsqrt(var + EPS) * lng_ref[g:g + 1, :] + lnb_ref[g:g + 1, :]).astype(BF16)
        wtri = jnp.where(tri, ws_ref[g], 0.0).astype(BF16)
        for c in range(n_chunks):
            r0, r1 = c * CHUNK, (c + 1) * CHUNK
            mixed = _dot(wtri, vn[r0:r1]) + bs_ref[:, lo:hi]
            ymix[r0:r1, lo:hi] = (u[r0:r1] * mixed).astype(BF16)

    for i in range(max(n_chunks, N_GROUPS)):
        if i < n_chunks:
            phase_b(i)
        if i < N_GROUPS:
            spatial_gating(i)

    for h in range(N_HEADS):
        lo, hi = h * LANES, (h + 1) * LANES
        ymix[:, GM_WIDTH + lo:GM_WIDTH + hi] = (_rms(o_s[:, lo:hi], onorm_ref[...])
                                                * _silu(gate_s[:, lo:hi])).astype(BF16)

    out_ref[0] = x + _dot(ymix[...], wout_ref[...])


def _mixer(x, l, params):
    b, s, d = x.shape
    ts = TS_MIX
    tile = pl.BlockSpec((1, ts, d), lambda i, j: (i, j, 0))
    consts = params
    return pl.pallas_call(
        _mixer_kernel,
        grid=(b, s // ts),
        in_specs=[tile] + [_layer_spec(a, l) for a in consts],
        out_specs=tile,
        out_shape=jax.ShapeDtypeStruct(x.shape, F32),
        scratch_shapes=[
            pltpu.VMEM((HIST + ts, 3 * DN_WIDTH), F32),
            pltpu.VMEM((N_HEADS, LANES, LANES), F32),
            pltpu.VMEM((ts, 3 * DN_WIDTH), F32),
            pltpu.VMEM((ts, LANES), F32),
            pltpu.VMEM((ts // CHUNK, LANES, CHUNK), F32),
            pltpu.VMEM((ts, DN_WIDTH), F32),
            pltpu.VMEM((ts, DN_WIDTH), BF16),
            pltpu.VMEM((ts, DN_WIDTH), BF16),
            pltpu.VMEM((ts, DN_WIDTH), BF16),
            pltpu.VMEM((ts // CHUNK, DN_WIDTH, CHUNK), BF16),
            pltpu.VMEM((ts, DN_WIDTH), F32),
            pltpu.VMEM((ts, GM_WIDTH + DN_WIDTH), BF16),
            pltpu.VMEM((ts, 2 * GM_WIDTH), F32),
            pltpu.VMEM((ts, DN_WIDTH), F32),
        ],
        compiler_params=pltpu.CompilerParams(dimension_semantics=("arbitrary", "arbitrary"),
                                             vmem_limit_bytes=VMEM_LIMIT),
        name="mixer",
    )(x, *consts)


def _xattn_kernel(x_ref, g_ref, wq_ref, k_ref, v_ref, wo_ref, out_ref, o_s):
    x = x_ref[0]
    d = x.shape[-1]
    hd = d // XA_HEADS
    hn = _rms(x, g_ref[...]).astype(BF16)
    q = _dot(hn, wq_ref[...]).astype(BF16)
    heads = [dict(lo=h * hd, hi=(h + 1) * hd) for h in range(XA_HEADS)]
    for u in heads:
        u["s"] = _dot_nt(q[:, u["lo"]:u["hi"]], k_ref[0, :, u["lo"]:u["hi"]]) * (hd ** -0.5)
    for u in heads:
        s = u.pop("s")
        e = jnp.exp(s - jnp.max(s, axis=-1, keepdims=True))
        u["p"] = (e / jnp.sum(e, axis=-1, keepdims=True)).astype(BF16)
    for u in heads:
        o_s[:, u["lo"]:u["hi"]] = _dot(u.pop("p"), v_ref[0, :, u["lo"]:u["hi"]]).astype(BF16)
    out_ref[0] = x + _dot(o_s[...], wo_ref[...])


def _xattn(x, l, g, wq, k, v, wo):
    b, s, d = x.shape
    nm = k.shape[2]
    ts = TS_XA
    tile = pl.BlockSpec((1, ts, d), lambda i, j: (i, j, 0))
    kv = pl.BlockSpec((None, 1, nm, d), lambda i, j: (l, i, 0, 0))
    return pl.pallas_call(
        _xattn_kernel,
        grid=(b, s // ts),
        in_specs=[tile, _layer_spec(g, l), _layer_spec(wq, l), kv, kv, _layer_spec(wo, l)],
        out_specs=tile,
        out_shape=jax.ShapeDtypeStruct(x.shape, F32),
        scratch_shapes=[pltpu.VMEM((ts, d), BF16)],
        compiler_params=pltpu.CompilerParams(dimension_semantics=("arbitrary", "arbitrary"),
                                             vmem_limit_bytes=VMEM_LIMIT),
        name="xattn",
    )(x, g, wq, k, v, wo)


def _ffn_kernel(x_ref, g_ref, w1_ref, w2_ref, gf_ref, out_ref, *, final_norm):
    x = x_ref[...]
    hn = _rms(x, g_ref[...]).astype(BF16)
    acc = x
    for j in range(w1_ref.shape[1] // FF_BLOCK):
        lo, hi = j * FF_BLOCK, (j + 1) * FF_BLOCK
        a = jnp.maximum(_dot(hn, w1_ref[:, lo:hi]), 0.0)
        acc = acc + _dot((a * a).astype(BF16), w2_ref[lo:hi, :])
    if final_norm:
        acc = _rms(acc, gf_ref[...])
    out_ref[...] = acc


def _ffn(x2, l, g, w1, w2, gf, final_norm):
    t, d = x2.shape
    tm = TM_FFN
    tile = pl.BlockSpec((tm, d), lambda i: (i, 0))
    return pl.pallas_call(
        functools.partial(_ffn_kernel, final_norm=final_norm),
        grid=(t // tm,),
        in_specs=[tile, _layer_spec(g, l), _layer_spec(w1, l), _layer_spec(w2, l), _layer_spec(gf, 0)],
        out_specs=tile,
        out_shape=jax.ShapeDtypeStruct(x2.shape, F32),
        compiler_params=pltpu.CompilerParams(dimension_semantics=("arbitrary",), vmem_limit_bytes=VMEM_LIMIT),
        name="ffn_final" if final_norm else "ffn",
    )(x2, g, w1, w2, gf)


def kernel(x, mem, norm_mix, w_in, gm_ln_g, gm_ln_b, gm_ws, gm_bs, dn_conv, dn_a_log, dn_dt_bias, dn_onorm,
           w_out, norm_xa, norm_mem, xa_wq, xa_wk, xa_wv, xa_wo, norm_ffn, ffn_w1, ffn_w2, norm_final):
    b, s, d = x.shape
    depth = w_in.shape[0]
    assert s % TS_MIX == 0 and s % TS_XA == 0 and (b * s) % TM_FFN == 0 and TS_MIX % CHUNK == 0
    assert w_in.shape[2] == C_AB + 2 * N_HEADS
    rows = lambda a: a.reshape(a.shape[0], 1, -1)
    w_in_p = jnp.pad(w_in, ((0, 0), (0, 0), (0, D_IN_PADDED - w_in.shape[2]))).astype(BF16)
    bs_full = jnp.repeat(jnp.swapaxes(gm_bs, 1, 2), LANES, axis=2)
    pad_heads = lambda a: jnp.pad(rows(a), ((0, 0), (0, 0), (N_HEADS, LANES - 2 * N_HEADS)))
    mixer_params = (rows(norm_mix), w_in_p, gm_ln_g, gm_ln_b, gm_ws, bs_full, dn_conv, pad_heads(dn_a_log),
                    pad_heads(dn_dt_bias), rows(dn_onorm), w_out.astype(BF16))
    wq, wk, wv, wo = (w.astype(BF16) for w in (xa_wq, xa_wk, xa_wv, xa_wo))
    w1, w2 = ffn_w1.astype(BF16), ffn_w2.astype(BF16)
    g_xa, g_mem, g_ffn, g_final = rows(norm_xa), rows(norm_mem), rows(norm_ffn), norm_final.reshape(1, 1, -1)
    k, v = _memkv(mem, g_mem, wk, wv)
    for l in range(depth):
        x = _mixer(x, l, mixer_params)
        x = _xattn(x, l, g_xa, wq, k, v, wo)
        x = _ffn(x.reshape(b * s, d), l, g_ffn, w1, w2, g_final, final_norm=(l == depth - 1)).reshape(b, s, d)
    return x
```
